```python
import math
import jax, jax.numpy as jnp
from jax import lax
import numpy as np

D_MODEL = 2048
BATCH = 4
SEQ = 2048
DEPTH = 2
DEC_BATCH = 128
DEC_SEQ = 8
PAST_LEN = 16384
PAGE_SIZE = 128

N_RWKV = (DEPTH + 1) // 2
N_HGRN = DEPTH // 2
RW_HEAD = 64
RW_HEADS = D_MODEL // RW_HEAD
RW_DECAY_LORA = 96
RW_AAA_LORA = 96
RW_GATE_LORA = 256
RW_GN_EPS = 64e-5
HG_KEY = 128
HG_HEADS = D_MODEL // HG_KEY
HG_VAL = D_MODEL // HG_HEADS
HG_CHUNK = 64
PK_HEADS = 8
PK_NKEYS = 128
PK_EXPERTS = PK_NKEYS * PK_NKEYS
PK_QDIM = 256
PK_HALF = PK_QDIM // 2
PK_TOPK = 16
PK_TOKEN_BLOCK = 128
RMS_EPS = 1e-6

kernel_name = "rwkv7_hgrn2_peer_hybrid_step"

F32 = jnp.float32


def rmsnorm(x, w):
    xf = x.astype(F32)
    y = xf * lax.rsqrt(jnp.mean(xf * xf, axis=-1, keepdims=True) + RMS_EPS)
    return (y * w.astype(F32)).astype(x.dtype)


def rwkv7_scan(r, decay, k, v, kk, a, S0):
    def step(S, inp):
        rt, wt, kt, vt, kkt, at = inp
        sa = jnp.einsum('bhvk,bhk->bhv', S, -kkt)
        S = S * wt[:, :, None, :] + sa[..., None] * (kkt * at)[:, :, None, :] + vt[..., None] * kt[:, :, None, :]
        y = jnp.einsum('bhvk,bhk->bhv', S, rt)
        return S, y
    xs = tuple(z.transpose(1, 0, 2, 3) for z in (r, decay, k, v, kk, a))
    S, y = lax.scan(step, S0, xs)
    return y.transpose(1, 0, 2, 3), S


def rwkv7_mix(h, shift0, S0, mu, w_rkv, w0, w1, w2, a0, a1, a2, g1, g2, k_k, k_a, r_k, ln_w, ln_b, w_o):
    B, T, D = h.shape
    x_prev = jnp.concatenate([shift0[:, None].astype(h.dtype), h[:, :-1]], axis=1)
    xx = x_prev - h
    xm = h[None] + xx[None] * mu[:, None, None, :]
    r, k, v = jnp.einsum('sbtd,sde->sbte', xm[:3], w_rkv)
    w = -jax.nn.softplus(-(w0 + jnp.tanh(xm[3] @ w1) @ w2).astype(F32)) - 0.5
    decay = jnp.exp(-jnp.exp(w))
    a = jax.nn.sigmoid((a0 + (xm[4] @ a1) @ a2).astype(F32))
    g = jax.nn.sigmoid(xm[5] @ g1) @ g2
    heads = lambda z: z.astype(F32).reshape(B, T, RW_HEADS, RW_HEAD)
    r, k, v, decay, a = map(heads, (r, k, v, decay, a))
    kk = k * k_k.astype(F32).reshape(RW_HEADS, RW_HEAD)
    kk = kk * lax.rsqrt(jnp.maximum(jnp.sum(kk * kk, axis=-1, keepdims=True), 1e-24))
    k = k * (1.0 + (a - 1.0) * k_a.astype(F32).reshape(RW_HEADS, RW_HEAD))
    y, S = rwkv7_scan(r, decay, k, v, kk, a, S0.astype(F32))
    mean = jnp.mean(y, axis=-1, keepdims=True)
    var = jnp.mean(jnp.square(y - mean), axis=-1, keepdims=True)
    y = (y - mean) * lax.rsqrt(var + RW_GN_EPS) * ln_w.astype(F32).reshape(RW_HEADS, RW_HEAD) \
        + ln_b.astype(F32).reshape(RW_HEADS, RW_HEAD)
    y = y + jnp.sum(r * k * r_k.astype(F32), axis=-1, keepdims=True) * v
    out = (y.reshape(B, T, D).astype(h.dtype) * g) @ w_o
    return out, S, h[:, -1]


def hgrn2_chunked(q, k, v, logf, S0, chunk):
    B, T, H, K = q.shape
    V = v.shape[-1]
    n = T // chunk
    to_chunks = lambda z: z.reshape(B, n, chunk, H, z.shape[-1]).transpose(1, 0, 3, 2, 4)
    mask = jnp.tril(jnp.ones((chunk, chunk), dtype=bool))

    def step(S, inp):
        qc, kc, vc, gc = inp
        b = jnp.cumsum(gc, axis=2)
        o_inter = jnp.einsum('bhck,bhkv->bhcv', qc * jnp.exp(b), S)
        diff = b[:, :, :, None, :] - b[:, :, None, :, :]
        dec = jnp.exp(jnp.where(mask[:, :, None], diff, -jnp.inf))
        A = jnp.einsum('bhtk,bhtsk,bhsk->bhts', qc, dec, kc)
        o = o_inter + jnp.einsum('bhts,bhsv->bhtv', A, vc)
        bC = b[:, :, -1]
        S = jnp.exp(bC)[..., None] * S + jnp.einsum('bhck,bhcv->bhkv', kc * jnp.exp(bC[:, :, None] - b), vc)
        return S, o

    S, o = lax.scan(step, S0, tuple(map(to_chunks, (q, k, v, logf))))
    return o.transpose(1, 0, 3, 2, 4).reshape(B, T, H, V), S


def hgrn2_mix(h, S0, lb, w_in, norm_w, w_o):
    B, T, D = h.shape
    proj = h @ w_in
    q, f, i, gate = jnp.split(proj, 4, axis=-1)
    q = jax.nn.silu(q.astype(F32)).reshape(B, T, HG_HEADS, HG_KEY)
    lbh = lb.reshape(HG_HEADS, HG_KEY)
    fg = lbh + (1.0 - lbh) * jax.nn.sigmoid(f.astype(F32).reshape(B, T, HG_HEADS, HG_KEY))
    logf = jnp.log(fg)
    kin = 1.0 - fg
    vin = i.astype(F32).reshape(B, T, HG_HEADS, HG_VAL)
    o, S = hgrn2_chunked(q, kin, vin, logf, S0.astype(F32), math.gcd(T, HG_CHUNK))
    o = o * lax.rsqrt(jnp.mean(o * o, axis=-1, keepdims=True) + RMS_EPS) * norm_w.astype(F32)
    o = o * jax.nn.silu(gate.astype(F32).reshape(B, T, HG_HEADS, HG_VAL))
    return o.reshape(B, T, D).astype(h.dtype) @ w_o, S


def peer(x, w_q, sub_keys, u, v):
    B, T, D = x.shape
    q = (x @ w_q).reshape(B, T, PK_HEADS, 2, PK_HALF)
    s = jnp.einsum('bthpd,hpnd->bthpn', q, sub_keys).astype(F32)
    s1, i1 = lax.top_k(s[..., 0, :], PK_TOPK)
    s2, i2 = lax.top_k(s[..., 1, :], PK_TOPK)
    cand = (s1[..., :, None] + s2[..., None, :]).reshape(B, T, PK_HEADS, PK_TOPK * PK_TOPK)
    cidx = (i1[..., :, None] * PK_NKEYS + i2[..., None, :]).reshape(B, T, PK_HEADS, PK_TOPK * PK_TOPK)
    top_s, pos = lax.top_k(cand, PK_TOPK)
    idx = jnp.take_along_axis(cidx, pos, axis=-1)
    gate = jax.nn.softmax(top_s, axis=-1).astype(x.dtype)
    n_tok = B * T
    E = PK_HEADS * PK_TOPK
    pad = (-n_tok) % PK_TOKEN_BLOCK
    xt = jnp.pad(x.reshape(n_tok, D), ((0, pad), (0, 0))).reshape(-1, PK_TOKEN_BLOCK, D)
    it = jnp.pad(idx.reshape(n_tok, E), ((0, pad), (0, 0))).reshape(-1, PK_TOKEN_BLOCK, E)
    gt = jnp.pad(gate.reshape(n_tok, E), ((0, pad), (0, 0))).reshape(-1, PK_TOKEN_BLOCK, E)

    def blk(args):
        xb, ib, gb = args
        hid = jax.nn.gelu(jnp.einsum('td,ted->te', xb, u[ib]), approximate=False)
        return jnp.einsum('te,ted->td', gb * hid, v[ib])

    y = lax.map(blk, (xt, it, gt))
    return y.reshape(-1, D)[:n_tok].reshape(B, T, D)


def trunk(x, wkv0, shift0, hg0,
          ln_mix_w, ln_ffn_w, ln_f_w,
          rw_mu, rw_w_rkv, rw_w0, rw_w1, rw_w2, rw_a0, rw_a1, rw_a2, rw_g1, rw_g2,
          rw_k_k, rw_k_a, rw_r_k, rw_ln_w, rw_ln_b, rw_w_o,
          hg_w_in, hg_lb, hg_norm_w, hg_w_o,
          pk_w_q, pk_keys, pk_u, pk_v):
    lb_all = jax.nn.softmax(hg_lb.astype(F32), axis=0)
    lb_all = jnp.cumsum(lb_all, axis=0) - lb_all[0]
    wkv_new, shift_new, hg_new = [], [], []
    for i in range(DEPTH):
        j = i // 2
        h = rmsnorm(x, ln_mix_w[i])
        if i % 2 == 0:
            out, S, last = rwkv7_mix(h, shift0[j], wkv0[j], rw_mu[j], rw_w_rkv[j], rw_w0[j], rw_w1[j], rw_w2[j],
                                     rw_a0[j], rw_a1[j], rw_a2[j], rw_g1[j], rw_g2[j], rw_k_k[j], rw_k_a[j],
                                     rw_r_k[j], rw_ln_w[j], rw_ln_b[j], rw_w_o[j])
            wkv_new.append(S)
            shift_new.append(last)
        else:
            out, S = hgrn2_mix(h, hg0[j], lb_all[i], hg_w_in[j], hg_norm_w[j], hg_w_o[j])
            hg_new.append(S)
        x = x + out
        x = x + peer(rmsnorm(x, ln_ffn_w[i]), pk_w_q[i], pk_keys[i], pk_u[i], pk_v[i])
    return rmsnorm(x, ln_f_w), jnp.stack(wkv_new), jnp.stack(shift_new), jnp.stack(hg_new)


def setup_inputs(seed: int = 0) -> dict:
    key = jax.random.key(seed)
    ks = iter(jax.random.split(key, 48))
    nrm = lambda shape, scale: jax.random.normal(next(ks), shape, F32) * scale
    uni = lambda shape, lo, hi: jax.random.uniform(next(ks), shape, F32, lo, hi)
    D = D_MODEL
    return {
        "x_prompt": nrm((BATCH, SEQ, D), 1.0),
        "x_sample": nrm((DEC_BATCH, DEC_SEQ, D), 1.0),
        "state_rwkv_wkv": nrm((N_RWKV, DEC_BATCH, RW_HEADS, RW_HEAD, RW_HEAD), 0.5),
        "state_rwkv_shift": nrm((N_RWKV, DEC_BATCH, D), 1.0),
        "state_hgrn": nrm((N_HGRN, DEC_BATCH, HG_HEADS, HG_KEY, HG_VAL), 0.5),
        "ln_mix_w": 1.0 + nrm((DEPTH, D), 0.02),
        "ln_ffn_w": 1.0 + nrm((DEPTH, D), 0.02),
        "ln_f_w": 1.0 + nrm((D,), 0.02),
        "rw_mu": uni((N_RWKV, 6, D), 0.0, 1.0),
        "rw_w_rkv": nrm((N_RWKV, 3, D, D), D ** -0.5),
        "rw_w0": uni((N_RWKV, D), -6.5, -1.5),
        "rw_w1": nrm((N_RWKV, D, RW_DECAY_LORA), D ** -0.5),
        "rw_w2": nrm((N_RWKV, RW_DECAY_LORA, D), 0.1 * RW_DECAY_LORA ** -0.5),
        "rw_a0": nrm((N_RWKV, D), 0.5),
        "rw_a1": nrm((N_RWKV, D, RW_AAA_LORA), D ** -0.5),
        "rw_a2": nrm((N_RWKV, RW_AAA_LORA, D), 0.1 * RW_AAA_LORA ** -0.5),
        "rw_g1": nrm((N_RWKV, D, RW_GATE_LORA), D ** -0.5),
        "rw_g2": nrm((N_RWKV, RW_GATE_LORA, D), RW_GATE_LORA ** -0.5),
        "rw_k_k": 0.85 + nrm((N_RWKV, D), 0.02),
        "rw_k_a": 1.0 + nrm((N_RWKV, D), 0.02),
        "rw_r_k": nrm((N_RWKV, RW_HEADS, RW_HEAD), 0.1),
        "rw_ln_w": 1.0 + nrm((N_RWKV, D), 0.02),
        "rw_ln_b": nrm((N_RWKV, D), 0.02),
        "rw_w_o": nrm((N_RWKV, D, D), D ** -0.5),
        "hg_w_in": nrm((N_HGRN, D, 4 * D), D ** -0.5),
        "hg_lb": nrm((DEPTH, D), 0.5),
        "hg_norm_w": 1.0 + nrm((N_HGRN, HG_VAL), 0.02),
        "hg_w_o": nrm((N_HGRN, D, D), D ** -0.5),
        "pk_w_q": nrm((DEPTH, D, PK_HEADS * PK_QDIM), D ** -0.5),
        "pk_keys": nrm((DEPTH, PK_HEADS, 2, PK_NKEYS, PK_HALF), PK_HALF ** -0.5),
        "pk_u": nrm((DEPTH, PK_EXPERTS, D), D ** -0.5),
        "pk_v": nrm((DEPTH, PK_EXPERTS, D), 0.3),
    }


def reference(x_prompt, x_sample, state_rwkv_wkv, state_rwkv_shift, state_hgrn,
              ln_mix_w, ln_ffn_w, ln_f_w,
              rw_mu, rw_w_rkv, rw_w0, rw_w1, rw_w2, rw_a0, rw_a1, rw_a2, rw_g1, rw_g2,
              rw_k_k, rw_k_a, rw_r_k, rw_ln_w, rw_ln_b, rw_w_o,
              hg_w_in, hg_lb, hg_norm_w, hg_w_o,
              pk_w_q, pk_keys, pk_u, pk_v):
    B = x_prompt.shape[0]
    wkv_p0 = jnp.zeros((N_RWKV, B, RW_HEADS, RW_HEAD, RW_HEAD), F32)
    shift_p0 = jnp.zeros((N_RWKV, B, D_MODEL), x_prompt.dtype)
    hg_p0 = jnp.zeros((N_HGRN, B, HG_HEADS, HG_KEY, HG_VAL), F32)
    y_prompt, wkv_p, shift_p, hg_p = trunk(
        x_prompt, wkv_p0, shift_p0, hg_p0,
        ln_mix_w, ln_ffn_w, ln_f_w,
        rw_mu, rw_w_rkv, rw_w0, rw_w1, rw_w2, rw_a0, rw_a1, rw_a2, rw_g1, rw_g2,
        rw_k_k, rw_k_a, rw_r_k, rw_ln_w, rw_ln_b, rw_w_o,
        hg_w_in, hg_lb, hg_norm_w, hg_w_o,
        pk_w_q, pk_keys, pk_u, pk_v)
    y_sample, wkv_s, shift_s, hg_s = trunk(
        x_sample, state_rwkv_wkv, state_rwkv_shift, state_hgrn,
        ln_mix_w, ln_ffn_w, ln_f_w,
        rw_mu, rw_w_rkv, rw_w0, rw_w1, rw_w2, rw_a0, rw_a1, rw_a2, rw_g1, rw_g2,
        rw_k_k, rw_k_a, rw_r_k, rw_ln_w, rw_ln_b, rw_w_o,
        hg_w_in, hg_lb, hg_norm_w, hg_w_o,
        pk_w_q, pk_keys, pk_u, pk_v)
    return (y_prompt, y_sample, wkv_p, shift_p, hg_p, wkv_s, shift_s, hg_s)
```

```python
import functools
import math

import jax
import jax.numpy as jnp
from jax import lax
from jax.experimental import pallas as pl
from jax.experimental.pallas import tpu as pltpu

F32 = jnp.float32
BF16 = jnp.bfloat16

D_MODEL = 2048
LANES = 128
N_CHUNK = D_MODEL // LANES
RW_HEAD = 64
HG_HEAD = 128
RW_GN_EPS = 64e-5
RMS_EPS = 1e-6
PK_HEADS = 8
PK_NKEYS = 128
PK_TOPK = 16
PK_EXPERTS = PK_NKEYS * PK_NKEYS
INV_SQRT2 = 1.0 / math.sqrt(2.0)
VMEM_LIMIT = 56 * 1024 * 1024


def _params(*sem):
    return pltpu.CompilerParams(dimension_semantics=sem, vmem_limit_bytes=VMEM_LIMIT)


def _row_tile(n, cap):
    t = cap
    while n % t:
        t //= 2
    return t


def _dot(a, b):
    return jnp.dot(a, b, preferred_element_type=F32)


def _rms(x, w):
    return x * lax.rsqrt(jnp.mean(x * x, axis=-1, keepdims=True) + RMS_EPS) * w


def _group_sum(x, ones_ref):
    tm, d = x.shape
    nch = d // LANES
    xs = jnp.concatenate([x[:, j * LANES:(j + 1) * LANES] for j in range(nch)], axis=0)
    hi = xs.astype(BF16)
    lo = (xs - hi.astype(F32)).astype(BF16)
    w = ones_ref[...]
    r = _dot(hi, w) + _dot(lo, w)
    return jnp.concatenate([r[j * tm:(j + 1) * tm] for j in range(nch)], axis=1)


def _block_ones(group, size=LANES):
    i = jnp.arange(size) // group
    return (i[:, None] == i[None, :]).astype(BF16)


def _rms_kernel(x_ref, w_ref, o_ref):
    o_ref[...] = _rms(x_ref[...], w_ref[...]).astype(o_ref.dtype)


def rmsnorm(x, w, out_dtype=F32):
    n, d = x.shape
    tm = _row_tile(n, 512)
    return pl.pallas_call(
        _rms_kernel,
        grid=(n // tm,),
        in_specs=[pl.BlockSpec((tm, d), lambda i: (i, 0)), pl.BlockSpec((1, d), lambda i: (0, 0))],
        out_specs=pl.BlockSpec((tm, d), lambda i: (i, 0)),
        out_shape=jax.ShapeDtypeStruct((n, d), out_dtype),
        compiler_params=_params("parallel"),
        name="rmsnorm",
    )(x, w.reshape(1, d))


def _addnorm_kernel(x_ref, y_ref, w_ref, s_ref, n_ref):
    s = x_ref[...] + y_ref[...]
    s_ref[...] = s
    n_ref[...] = _rms(s, w_ref[...]).astype(n_ref.dtype)


def add_norm(x, y, w, norm_dtype):
    n, d = x.shape
    tm = _row_tile(n, 512)
    row = pl.BlockSpec((tm, d), lambda i: (i, 0))
    return pl.pallas_call(
        _addnorm_kernel,
        grid=(n // tm,),
        in_specs=[row, row, pl.BlockSpec((1, d), lambda i: (0, 0))],
        out_specs=[row, row],
        out_shape=[jax.ShapeDtypeStruct((n, d), F32), jax.ShapeDtypeStruct((n, d), norm_dtype)],
        compiler_params=_params("parallel"),
        name="add_norm",
    )(x, y, w.reshape(1, d))


def _rkv_kernel(h_ref, xp_ref, mu_ref, w_ref, o_ref):
    h = h_ref[...]
    xm = h + (xp_ref[...] - h) * mu_ref[0]
    o_ref[0] = _dot(xm.astype(BF16), w_ref[0])


def rwkv_rkv(h, xp, mu3, w3):
    n, d = h.shape
    tm = _row_tile(n, 512)
    row = pl.BlockSpec((tm, d), lambda s, i: (i, 0))
    return pl.pallas_call(
        _rkv_kernel,
        grid=(3, n // tm),
        in_specs=[row, row,
                  pl.BlockSpec((1, 1, d), lambda s, i: (s, 0, 0)),
                  pl.BlockSpec((1, d, d), lambda s, i: (s, 0, 0))],
        out_specs=pl.BlockSpec((1, tm, d), lambda s, i: (s, i, 0)),
        out_shape=jax.ShapeDtypeStruct((3, n, d), F32),
        compiler_params=_params("arbitrary", "arbitrary"),
        name="rwkv_rkv",
    )(h, xp, mu3.reshape(3, 1, d), w3)


def _softplus(z):
    return jnp.maximum(z, 0.0) + jnp.log1p(jnp.exp(-jnp.abs(z)))


def _rwprep_kernel(h_ref, xp_ref, r_ref, k_ref, mu_ref, w1_ref, w2_ref, a1_ref, a2_ref, g1_ref, g2_ref,
                   w0_ref, a0_ref, kk_ref, ka_ref, rk_ref, ones_ref,
                   dec_o, nkk_o, b_o, k2_o, g_o, rkb_o):
    h = h_ref[...]
    dx = xp_ref[...] - h
    xw = (h + dx * mu_ref[0]).astype(BF16)
    xa = (h + dx * mu_ref[1]).astype(BF16)
    xg = (h + dx * mu_ref[2]).astype(BF16)
    wl = w0_ref[...] + _dot(jnp.tanh(_dot(xw, w1_ref[...])).astype(BF16), w2_ref[...])
    wv = -_softplus(-wl) - 0.5
    dec_o[...] = jnp.exp(-jnp.exp(wv))
    a = jax.nn.sigmoid(a0_ref[...] + _dot(_dot(xa, a1_ref[...]).astype(BF16), a2_ref[...]))
    g_o[...] = _dot(jax.nn.sigmoid(_dot(xg, g1_ref[...])).astype(BF16), g2_ref[...])
    k = k_ref[...]
    kk = k * kk_ref[...]
    kk = kk * lax.rsqrt(jnp.maximum(_group_sum(kk * kk, ones_ref), 1e-24))
    k2 = k * (1.0 + (a - 1.0) * ka_ref[...])
    nkk_o[...] = -kk
    b_o[...] = kk * a
    k2_o[...] = k2
    rkb_o[...] = _group_sum(r_ref[...] * k2 * rk_ref[...], ones_ref)


def rwkv_prep(h, xp, r, k, mu3, w1, w2, a1, a2, g1, g2, w0, a0, k_k, k_a, r_k, ones64):
    n, d = h.shape
    tm = _row_tile(n, 128)
    row = pl.BlockSpec((tm, d), lambda i: (i, 0))
    vec = pl.BlockSpec((1, d), lambda i: (0, 0))

    def full(a):
        return pl.BlockSpec(a.shape, lambda i: (0,) * a.ndim)

    mu3 = mu3.reshape(3, 1, d)
    vecs = [v.reshape(1, d) for v in (w0, a0, k_k, k_a, r_k)]
    return pl.pallas_call(
        _rwprep_kernel,
        grid=(n // tm,),
        in_specs=[row, row, row, row, full(mu3), full(w1), full(w2), full(a1), full(a2), full(g1), full(g2),
                  vec, vec, vec, vec, vec, full(ones64)],
        out_specs=[row] * 6,
        out_shape=[jax.ShapeDtypeStruct((n, d), F32)] * 6,
        compiler_params=_params("parallel"),
        name="rwkv_prep",
    )(h, xp, r, k, mu3, w1, w2, a1, a2, g1, g2, *vecs, ones64)


def _rwpost_kernel(y_ref, v_ref, rkb_ref, g_ref, x_ref, lnw_ref, lnb_ref, wo_ref, nw_ref, ones_ref,
                   x1_o, xn_o):
    y = y_ref[...]
    d = y - _group_sum(y, ones_ref) * (1.0 / RW_HEAD)
    var = _group_sum(d * d, ones_ref) * (1.0 / RW_HEAD)
    z = d * lax.rsqrt(var + RW_GN_EPS) * lnw_ref[...] + lnb_ref[...] + rkb_ref[...] * v_ref[...]
    x1 = x_ref[...] + _dot((z * g_ref[...]).astype(BF16), wo_ref[...])
    x1_o[...] = x1
    xn_o[...] = _rms(x1, nw_ref[...]).astype(xn_o.dtype)


def rwkv_post(y, v, rkb, g, x, ln_w, ln_b, w_o, norm_w, ones64):
    n, d = y.shape
    tm = _row_tile(n, 128)
    row = pl.BlockSpec((tm, d), lambda i: (i, 0))
    vec = pl.BlockSpec((1, d), lambda i: (0, 0))
    return pl.pallas_call(
        _rwpost_kernel,
        grid=(n // tm,),
        in_specs=[row, row, row, row, row, vec, vec, pl.BlockSpec((d, d), lambda i: (0, 0)), vec,
                  pl.BlockSpec((LANES, LANES), lambda i: (0, 0))],
        out_specs=[row, row],
        out_shape=[jax.ShapeDtypeStruct((n, d), F32), jax.ShapeDtypeStruct((n, d), BF16)],
        compiler_params=_params("parallel"),
        name="rwkv_post",
    )(y, v, rkb, g, x, ln_w.reshape(1, d), ln_b.reshape(1, d), w_o, norm_w.reshape(1, d), ones64)


def _hgin_kernel(kind, a_ref, w_ref, lb_ref, o_ref):
    o = _dot(a_ref[...], w_ref[...])
    if kind == "silu":
        o = o * jax.nn.sigmoid(o)
    elif kind == "forget":
        lb = lb_ref[...]
        o = lb + (1.0 - lb) * jax.nn.sigmoid(o)
    o_ref[...] = o


def hgrn_in(a, w, lb, kind):
    n, d = a.shape
    tm = _row_tile(n, 512)
    return pl.pallas_call(
        functools.partial(_hgin_kernel, kind),
        grid=(n // tm,),
        in_specs=[pl.BlockSpec((tm, d), lambda i: (i, 0)), pl.BlockSpec((d, d), lambda i: (0, 0)),
                  pl.BlockSpec((1, d), lambda i: (0, 0))],
        out_specs=pl.BlockSpec((tm, d), lambda i: (i, 0)),
        out_shape=jax.ShapeDtypeStruct((n, d), F32),
        compiler_params=_params("parallel"),
        name="hgrn_in_" + kind,
    )(a, w, lb.reshape(1, d))


def _hgpost_kernel(o_ref, sg_ref, x_ref, nw_ref, wo_ref, fw_ref, ones_ref, x3_o, xn_o):
    o = o_ref[...]
    ms = _group_sum(o * o, ones_ref) * (1.0 / HG_HEAD)
    z = o * lax.rsqrt(ms + RMS_EPS) * nw_ref[...] * sg_ref[...]
    x3 = x_ref[...] + _dot(z.astype(BF16), wo_ref[...])
    x3_o[...] = x3
    xn_o[...] = _rms(x3, fw_ref[...]).astype(xn_o.dtype)


def hgrn_post(o, sgate, x, norm_w_tiled, w_o, ffn_w, ones128):
    n, d = o.shape
    tm = _row_tile(n, 256)
    row = pl.BlockSpec((tm, d), lambda i: (i, 0))
    vec = pl.BlockSpec((1, d), lambda i: (0, 0))
    return pl.pallas_call(
        _hgpost_kernel,
        grid=(n // tm,),
        in_specs=[row, row, row, vec, pl.BlockSpec((d, d), lambda i: (0, 0)), vec,
                  pl.BlockSpec((LANES, LANES), lambda i: (0, 0))],
        out_specs=[row, row],
        out_shape=[jax.ShapeDtypeStruct((n, d), F32), jax.ShapeDtypeStruct((n, d), BF16)],
        compiler_params=_params("parallel"),
        name="hgrn_post",
    )(o, sgate, x, norm_w_tiled.reshape(1, d), w_o, ffn_w.reshape(1, d), ones128)


def _scan_kernel(mode, bg, tb, *refs):
    rwkv = mode == "rwkv"
    rows = RW_HEAD if rwkv else HG_HEAD
    nc = bg * N_CHUNK
    sbl = min(rows, tb)
    nsb = tb // sbl
    if rwkv:
        (w_ref, nkk_ref, b_ref, k_ref, r_ref, v_ref, s0_ref, ones_ref,
         y_ref, st_ref, s_scr, vt_scr, yt_scr, l1, r1, l2, r2) = refs
    else:
        (w_ref, r_ref, v_ref, s0_ref, ones_ref,
         y_ref, st_ref, s_scr, vt_scr, yt_scr, l1, r1, l2, r2) = refs
    tblk = pl.program_id(1)
    lane = lax.broadcasted_iota(jnp.int32, (rows, LANES), 1)
    low = lane < RW_HEAD

    @pl.when(tblk == 0)
    def _():
        for c in range(nc):
            s_scr[c] = s0_ref[c // N_CHUNK, c % N_CHUNK]

    def chain(c):
        return c // N_CHUNK, slice((c % N_CHUNK) * LANES, (c % N_CHUNK + 1) * LANES)

    def row(ref, c, t):
        return ref[c // N_CHUNK, t, c % N_CHUNK:c % N_CHUNK + 1, :]

    def pair_slot(c):
        return slice((c // 2) * rows, (c // 2 + 1) * rows), slice((c % 2) * LANES, (c % 2 + 1) * LANES)

    for c in range(nc):
        bi, sl = chain(c)
        vb = v_ref[bi, :, sl]
        if tb < LANES:
            vb = jnp.concatenate([vb, jnp.zeros((LANES - tb, LANES), F32)], axis=0)
        vt = vb.T
        if rwkv:
            h0, h1 = vt[:RW_HEAD], vt[RW_HEAD:]
            vt_scr[c, 0] = jnp.where(low, h0, pltpu.roll(h1, RW_HEAD, 1))
            if nsb == 2:
                vt_scr[c, 1] = jnp.where(low, pltpu.roll(h0, RW_HEAD, 1), h1)
        else:
            vt_scr[c, 0] = vt
        for sb in range(nsb):
            yt_scr[c, sb] = jnp.zeros((rows, LANES), F32)

    def step(t, carry):
        sb = lax.shift_right_logical(t, int(math.log2(sbl))) if nsb > 1 else 0
        onehot = (lane & (rows - 1)) == (t & (sbl - 1))
        for c in range(nc):
            bi, sl = chain(c)
            vsel = jnp.where(onehot, vt_scr[c, sb], 0.0).astype(BF16)
            if rwkv:
                p = (s_scr[c] * row(nkk_ref, c, t)).astype(BF16)
                l1[c * rows:(c + 1) * rows, :] = jnp.concatenate([p, vsel], axis=1)
            else:
                rs, ls = pair_slot(c)
                l1[rs, ls] = vsel
        r1[...] = _dot(l1[...], ones_ref[...])
        for c in range(nc):
            bi, sl = chain(c)
            rs, ls = pair_slot(c)
            wrow = row(w_ref, c, t)
            if rwkv:
                sa = r1[c * rows:(c + 1) * rows, 0:LANES]
                vc = r1[c * rows:(c + 1) * rows, LANES:2 * LANES]
                s_new = (s_scr[c] * wrow + sa * row(b_ref, c, t)
                         + vc * row(k_ref, c, t))
            else:
                s_new = s_scr[c] * wrow + r1[rs, ls] * (1.0 - wrow)
            s_scr[c] = s_new
            l2[rs, ls] = (s_new * row(r_ref, c, t)).astype(BF16)
        r2[...] = _dot(l2[...], ones_ref[...])
        for c in range(nc):
            rs, ls = pair_slot(c)
            yt_scr[c, sb] = jnp.where(onehot, r2[rs, ls], yt_scr[c, sb])
        return carry

    lax.fori_loop(0, tb, step, 0)

    for c in range(nc):
        bi, sl = chain(c)
        if rwkv:
            y0 = yt_scr[c, 0]
            y1 = yt_scr[c, 1] if nsb == 2 else jnp.zeros((rows, LANES), F32)
            yt = jnp.concatenate([jnp.where(low, y0, pltpu.roll(y1, RW_HEAD, 1)),
                                  jnp.where(low, pltpu.roll(y0, RW_HEAD, 1), y1)], axis=0)
        else:
            yt = yt_scr[c, 0]
        y_ref[bi, :, sl] = yt.T[:tb]

    @pl.when(tblk == pl.num_programs(1) - 1)
    def _():
        for c in range(nc):
            st_ref[c // N_CHUNK, c % N_CHUNK] = s_scr[c]


def recurrence(mode, row_inputs, v, s0, ones, bg, tb):
    b, t, d = v.shape
    rows = s0.shape[2]
    nc = bg * N_CHUNK
    sbl = min(rows, tb)
    nsb = tb // sbl
    seq = pl.BlockSpec((bg, tb, d), lambda i, j: (i, j, 0))
    seq4 = pl.BlockSpec((bg, tb, N_CHUNK, LANES), lambda i, j: (i, j, 0, 0))
    state = pl.BlockSpec((bg, N_CHUNK, rows, LANES), lambda i, j: (i, 0, 0, 0))
    l1_rows = nc * rows if mode == "rwkv" else nc * rows // 2
    row_inputs = [a.reshape(b, t, N_CHUNK, LANES) for a in row_inputs] + [v]
    return pl.pallas_call(
        functools.partial(_scan_kernel, mode, bg, tb),
        grid=(b // bg, t // tb),
        in_specs=[seq4] * (len(row_inputs) - 1) + [seq, state,
                                                   pl.BlockSpec((2 * LANES, 2 * LANES), lambda i, j: (0, 0))],
        out_specs=[seq, state],
        out_shape=[jax.ShapeDtypeStruct((b, t, d), F32), jax.ShapeDtypeStruct(s0.shape, F32)],
        scratch_shapes=[pltpu.VMEM((nc, rows, LANES), F32),
                        pltpu.VMEM((nc, nsb, rows, LANES), F32),
                        pltpu.VMEM((nc, nsb, rows, LANES), F32),
                        pltpu.VMEM((l1_rows, 2 * LANES), BF16),
                        pltpu.VMEM((l1_rows, 2 * LANES), F32),
                        pltpu.VMEM((nc * rows // 2, 2 * LANES), BF16),
                        pltpu.VMEM((nc * rows // 2, 2 * LANES), F32)],
        compiler_params=_params("parallel", "arbitrary"),
        name="scan_" + mode,
    )(*row_inputs, s0, ones)


def _top_values(s, count):
    n = s.shape[0]
    rid = lax.broadcasted_iota(jnp.int32, s.shape, 0)
    vals = []
    for _ in range(count):
        m = jnp.max(s, axis=0, keepdims=True)
        vals.append(m)
        first = jnp.min(jnp.where(s == m, rid, n), axis=0, keepdims=True)
        s = jnp.where(rid == first, -jnp.inf, s)
    return vals


_CAND_PAIRS = [(a, b) for a in range(PK_TOPK) for b in range(PK_TOPK) if (a + 1) * (b + 1) <= PK_TOPK]
_CAND_ROWS = -(-len(_CAND_PAIRS) // 8) * 8


def _pquery_kernel(xn_ref, wq_ref, keys_ref, st_ref, thr_ref, cand_scr):
    q = _dot(xn_ref[...], wq_ref[...])
    for c in range(2 * PK_HEADS):
        qc = q[:, c * LANES:(c + 1) * LANES].astype(BF16)
        st_ref[c * LANES:(c + 1) * LANES, :] = lax.dot_general(
            keys_ref[c], qc, (((1,), (1,)), ((), ())), preferred_element_type=F32)
    for h in range(PK_HEADS):
        t1 = _top_values(st_ref[(2 * h) * LANES:(2 * h + 1) * LANES, :], PK_TOPK)
        t2 = _top_values(st_ref[(2 * h + 1) * LANES:(2 * h + 2) * LANES, :], PK_TOPK)
        cand_scr[...] = jnp.full(cand_scr.shape, -jnp.inf, F32)
        for i, (a, b) in enumerate(_CAND_PAIRS):
            cand_scr[i:i + 1, :] = t1[a] + t2[b]
        top = _top_values(cand_scr[...], PK_TOPK)
        z = jnp.zeros_like(top[0])
        for m in top:
            z = z + jnp.exp(m - top[0])
        thr_ref[2 * h:2 * h + 1, :] = top[-1]
        thr_ref[2 * h + 1:2 * h + 2, :] = top[0] + jnp.log(z)


def peer_query(xn, w_q, keys):
    n, d = xn.shape
    tm = _row_tile(n, 256)
    return pl.pallas_call(
        _pquery_kernel,
        grid=(n // tm,),
        in_specs=[pl.BlockSpec((tm, d), lambda i: (i, 0)), pl.BlockSpec((d, d), lambda i: (0, 0)),
                  pl.BlockSpec(keys.shape, lambda i: (0, 0, 0))],
        out_specs=[pl.BlockSpec((d, tm), lambda i: (0, i)), pl.BlockSpec((2 * PK_HEADS, tm), lambda i: (0, i))],
        out_shape=[jax.ShapeDtypeStruct((d, n), F32), jax.ShapeDtypeStruct((2 * PK_HEADS, n), F32)],
        scratch_shapes=[pltpu.VMEM((_CAND_ROWS, tm), F32)],
        compiler_params=_params("parallel"),
        name="peer_query",
    )(xn, w_q, keys)


def _pdense_kernel(ec, xn_ref, u_ref, vt_ref, st_ref, thr_ref, y_ref, acc, p_scr):
    j = pl.program_id(1)
    groups = ec // PK_NKEYS

    @pl.when(j == 0)
    def _():
        acc[...] = jnp.zeros(acc.shape, F32)

    hid = lax.dot_general(u_ref[...], xn_ref[...], (((1,), (1,)), ((), ())), preferred_element_type=F32)
    for ii in range(groups):
        gate = None
        for h in range(PK_HEADS):
            s1 = st_ref[pl.ds(2 * h * LANES + j * groups + ii, 1), :]
            c = s1 + st_ref[(2 * h + 1) * LANES:(2 * h + 2) * LANES, :]
            e = jnp.where(c >= thr_ref[2 * h:2 * h + 1, :], jnp.exp(c - thr_ref[2 * h + 1:2 * h + 2, :]), 0.0)
            gate = e if gate is None else gate + e
        hh = hid[ii * PK_NKEYS:(ii + 1) * PK_NKEYS, :]
        act = 0.5 * hh * (1.0 + lax.erf(hh * INV_SQRT2))
        p_scr[ii * PK_NKEYS:(ii + 1) * PK_NKEYS, :] = (act * gate).astype(BF16)
    acc[...] += _dot(vt_ref[...], p_scr[...])

    @pl.when(j == pl.num_programs(1) - 1)
    def _():
        y_ref[...] = acc[...].T


def peer_dense(xn, u, v_t, st, thr):
    n, d = xn.shape
    tm = _row_tile(n, 512)
    ec = 1024
    return pl.pallas_call(
        functools.partial(_pdense_kernel, ec),
        grid=(n // tm, PK_EXPERTS // ec),
        in_specs=[pl.BlockSpec((tm, d), lambda i, j: (i, 0)),
                  pl.BlockSpec((ec, d), lambda i, j: (j, 0)),
                  pl.BlockSpec((d, ec), lambda i, j: (0, j)),
                  pl.BlockSpec((d, tm), lambda i, j: (0, i)),
                  pl.BlockSpec((2 * PK_HEADS, tm), lambda i, j: (0, i))],
        out_specs=pl.BlockSpec((tm, d), lambda i, j: (i, 0)),
        out_shape=jax.ShapeDtypeStruct((n, d), F32),
        scratch_shapes=[pltpu.VMEM((d, tm), F32), pltpu.VMEM((ec, tm), BF16)],
        compiler_params=_params("parallel", "arbitrary"),
        name="peer_dense",
    )(xn, u, v_t, st, thr)


def peer(xn, w_q, keys, u, v_t):
    st, thr = peer_query(xn, w_q, keys)
    return peer_dense(xn, u, v_t, st, thr)


def _pack_rwkv_state(s):
    b = s.shape[0]
    return s.reshape(b, N_CHUNK, 2, RW_HEAD, RW_HEAD).transpose(0, 1, 3, 2, 4).reshape(b, N_CHUNK, RW_HEAD, LANES)


def _unpack_rwkv_state(s):
    b = s.shape[0]
    return s.reshape(b, N_CHUNK, RW_HEAD, 2, RW_HEAD).transpose(0, 1, 3, 2, 4).reshape(b, 2 * N_CHUNK, RW_HEAD, RW_HEAD)


def _shifted(h, shift0):
    return jnp.concatenate([shift0[:, None].astype(h.dtype), h[:, :-1]], axis=1)


def kernel(x_prompt, x_sample, state_rwkv_wkv, state_rwkv_shift, state_hgrn, ln_mix_w, ln_ffn_w, ln_f_w, rw_mu, rw_w_rkv, rw_w0, rw_w1, rw_w2, rw_a0, rw_a1, rw_a2, rw_g1, rw_g2, rw_k_k, rw_k_a, rw_r_k, rw_ln_w, rw_ln_b, rw_w_o, hg_w_in, hg_lb, hg_norm_w, hg_w_o, pk_w_q, pk_keys, pk_u, pk_v):
    bp, tp, d = x_prompt.shape
    bs, ts, _ = x_sample.shape
    n_p, n_s = bp * tp, bs * ts
    ones64 = _block_ones(RW_HEAD)
    ones128 = _block_ones(HG_HEAD)
    ones64x4 = _block_ones(RW_HEAD, 2 * LANES)
    ones128x2 = _block_ones(HG_HEAD, 2 * LANES)

    def split(a):
        return a[:n_p].reshape(bp, tp, d), a[n_p:].reshape(bs, ts, d)

    def both(fn, a_list, s0_p, s0_s):
        parts = [split(a) for a in a_list]
        yp, sp = fn([p[0] for p in parts], s0_p, True)
        ys, ss = fn([p[1] for p in parts], s0_s, False)
        return jnp.concatenate([yp.reshape(n_p, d), ys.reshape(n_s, d)], axis=0), sp, ss

    x = jnp.concatenate([x_prompt.reshape(n_p, d), x_sample.reshape(n_s, d)], axis=0)

    h = rmsnorm(x, ln_mix_w[0])
    h_p, h_s = split(h)
    xp = jnp.concatenate([_shifted(h_p, jnp.zeros((bp, d), F32)).reshape(n_p, d),
                          _shifted(h_s, state_rwkv_shift[0]).reshape(n_s, d)], axis=0)
    rkv = rwkv_rkv(h, xp, rw_mu[0, :3], rw_w_rkv[0].astype(BF16))
    lora_pad = LANES - rw_w1.shape[-1]
    pad_in = lambda w: jnp.pad(w, ((0, 0), (0, lora_pad))).astype(BF16)
    pad_out = lambda w: jnp.pad(w, ((0, lora_pad), (0, 0))).astype(BF16)
    dec, nkk, bb, k2, g, rkb = rwkv_prep(
        h, xp, rkv[0], rkv[1], rw_mu[0, 3:], pad_in(rw_w1[0]), pad_out(rw_w2[0]), pad_in(rw_a1[0]),
        pad_out(rw_a2[0]), rw_g1[0].astype(BF16), rw_g2[0].astype(BF16), rw_w0[0], rw_a0[0], rw_k_k[0],
        rw_k_a[0], rw_r_k[0].reshape(d), ones64)

    def rw_scan(rows_in, s0, prompt):
        return recurrence("rwkv", rows_in[:-1], rows_in[-1], s0, ones64x4, 2 if prompt else 4,
                          min(LANES, rows_in[0].shape[1]))

    y, wkv_p, wkv_s = both(rw_scan, [dec, nkk, bb, k2, rkv[0], rkv[2]],
                           jnp.zeros((bp, N_CHUNK, RW_HEAD, LANES), F32), _pack_rwkv_state(state_rwkv_wkv[0]))
    x, xn = rwkv_post(y, rkv[2], rkb, g, x, rw_ln_w[0], rw_ln_b[0], rw_w_o[0].astype(BF16), ln_ffn_w[0], ones64)
    yp = peer(xn, pk_w_q[0].astype(BF16), pk_keys[0].reshape(2 * PK_HEADS, PK_NKEYS, LANES).astype(BF16),
              pk_u[0].astype(BF16), pk_v[0].astype(BF16).T)
    x, h1 = add_norm(x, yp, ln_mix_w[1], BF16)

    lb = jax.nn.softmax(hg_lb.astype(F32), axis=0)
    lb = (jnp.cumsum(lb, axis=0) - lb[0])[1]
    w_in = hg_w_in[0].astype(BF16)
    q = hgrn_in(h1, w_in[:, :d], lb, "silu")
    f = hgrn_in(h1, w_in[:, d:2 * d], lb, "forget")
    vin = hgrn_in(h1, w_in[:, 2 * d:3 * d], lb, "none")
    sgate = hgrn_in(h1, w_in[:, 3 * d:], lb, "silu")

    def hg_scan(rows_in, s0, prompt):
        return recurrence("hgrn", rows_in[:-1], rows_in[-1], s0, ones128x2, 2 if prompt else 4,
                          min(LANES, rows_in[0].shape[1]))

    o, hg_p, hg_s = both(hg_scan, [f, q, vin], jnp.zeros((bp, N_CHUNK, HG_HEAD, LANES), F32),
                         jnp.swapaxes(state_hgrn[0], -1, -2))
    x, xn = hgrn_post(o, sgate, x, jnp.tile(hg_norm_w[0], N_CHUNK), hg_w_o[0].astype(BF16), ln_ffn_w[1], ones128)
    yp = peer(xn, pk_w_q[1].astype(BF16), pk_keys[1].reshape(2 * PK_HEADS, PK_NKEYS, LANES).astype(BF16),
              pk_u[1].astype(BF16), pk_v[1].astype(BF16).T)
    _, out = add_norm(x, yp, ln_f_w, F32)

    y_p, y_s = split(out)
    return (y_p, y_s,
            _unpack_rwkv_state(wkv_p)[None], h_p[:, -1][None], jnp.swapaxes(hg_p, -1, -2)[None],
            _unpack_rwkv_state(wkv_s)[None], h_s[:, -1][None], jnp.swapaxes(hg_s, -1, -2)[None])
```

```python
import functools
import math

import jax
import jax.numpy as jnp
from jax import lax
from jax.experimental import pallas as pl
from jax.experimental.pallas import tpu as pltpu

F32 = jnp.float32
BF16 = jnp.bfloat16

D_MODEL = 2048
LANES = 128
N_CHUNK = D_MODEL // LANES
RW_HEAD = 64
HG_HEAD = 128
RW_GN_EPS = 64e-5
RMS_EPS = 1e-6
PK_HEADS = 8
PK_NKEYS = 128
PK_TOPK = 16
PK_EXPERTS = PK_NKEYS * PK_NKEYS
INV_SQRT2 = 1.0 / math.sqrt(2.0)
LN2 = math.log(2.0)
LOG2E = 1.0 / LN2
VMEM_LIMIT = 56 * 1024 * 1024


def _params(*sem):
    return pltpu.CompilerParams(dimension_semantics=sem, vmem_limit_bytes=VMEM_LIMIT)


def _row_tile(n, cap):
    t = cap
    while n % t:
        t //= 2
    return t


def _dot(a, b):
    return jnp.dot(a, b, preferred_element_type=F32)


def _rms(x, w):
    return x * lax.rsqrt(jnp.mean(x * x, axis=-1, keepdims=True) + RMS_EPS) * w


def _group_sum(x, ones_ref):
    tm, d = x.shape
    nch = d // LANES
    xs = jnp.concatenate([x[:, j * LANES:(j + 1) * LANES] for j in range(nch)], axis=0)
    hi = xs.astype(BF16)
    lo = (xs - hi.astype(F32)).astype(BF16)
    w = ones_ref[...]
    r = _dot(hi, w) + _dot(lo, w)
    return jnp.concatenate([r[j * tm:(j + 1) * tm] for j in range(nch)], axis=1)


def _block_ones(group, size=LANES):
    i = jnp.arange(size) // group
    return (i[:, None] == i[None, :]).astype(BF16)


def _rms_kernel(x_ref, w_ref, o_ref):
    o_ref[...] = _rms(x_ref[...], w_ref[...]).astype(o_ref.dtype)


def rmsnorm(x, w, out_dtype=F32):
    n, d = x.shape
    tm = _row_tile(n, 512)
    return pl.pallas_call(
        _rms_kernel,
        grid=(n // tm,),
        in_specs=[pl.BlockSpec((tm, d), lambda i: (i, 0)), pl.BlockSpec((1, d), lambda i: (0, 0))],
        out_specs=pl.BlockSpec((tm, d), lambda i: (i, 0)),
        out_shape=jax.ShapeDtypeStruct((n, d), out_dtype),
        compiler_params=_params("parallel"),
        name="rmsnorm",
    )(x, w.reshape(1, d))


def _addnorm_kernel(x_ref, y_ref, w_ref, s_ref, n_ref):
    s = x_ref[...] + y_ref[...]
    s_ref[...] = s
    n_ref[...] = _rms(s, w_ref[...]).astype(n_ref.dtype)


def add_norm(x, y, w, norm_dtype):
    n, d = x.shape
    tm = _row_tile(n, 512)
    row = pl.BlockSpec((tm, d), lambda i: (i, 0))
    return pl.pallas_call(
        _addnorm_kernel,
        grid=(n // tm,),
        in_specs=[row, row, pl.BlockSpec((1, d), lambda i: (0, 0))],
        out_specs=[row, row],
        out_shape=[jax.ShapeDtypeStruct((n, d), F32), jax.ShapeDtypeStruct((n, d), norm_dtype)],
        compiler_params=_params("parallel"),
        name="add_norm",
    )(x, y, w.reshape(1, d))


def _rkv_kernel(h_ref, xp_ref, mu_ref, w_ref, o_ref):
    h = h_ref[...]
    xm = h + (xp_ref[...] - h) * mu_ref[0]
    o_ref[0] = _dot(xm.astype(BF16), w_ref[0])


def rwkv_rkv(h, xp, mu3, w3):
    n, d = h.shape
    tm = _row_tile(n, 512)
    row = pl.BlockSpec((tm, d), lambda s, i: (i, 0))
    return pl.pallas_call(
        _rkv_kernel,
        grid=(3, n // tm),
        in_specs=[row, row,
                  pl.BlockSpec((1, 1, d), lambda s, i: (s, 0, 0)),
                  pl.BlockSpec((1, d, d), lambda s, i: (s, 0, 0))],
        out_specs=pl.BlockSpec((1, tm, d), lambda s, i: (s, i, 0)),
        out_shape=jax.ShapeDtypeStruct((3, n, d), F32),
        compiler_params=_params("arbitrary", "arbitrary"),
        name="rwkv_rkv",
    )(h, xp, mu3.reshape(3, 1, d), w3)


def _softplus(z):
    return jnp.maximum(z, 0.0) + jnp.log1p(jnp.exp(-jnp.abs(z)))


def _rwprep_kernel(h_ref, xp_ref, r_ref, k_ref, mu_ref, w1_ref, w2_ref, a1_ref, a2_ref, g1_ref, g2_ref,
                   w0_ref, a0_ref, kk_ref, ka_ref, rk_ref, ones_ref,
                   dec_o, nkk_o, b_o, k2_o, g_o, rkb_o):
    h = h_ref[...]
    dx = xp_ref[...] - h
    xw = (h + dx * mu_ref[0]).astype(BF16)
    xa = (h + dx * mu_ref[1]).astype(BF16)
    xg = (h + dx * mu_ref[2]).astype(BF16)
    wl = w0_ref[...] + _dot(jnp.tanh(_dot(xw, w1_ref[...])).astype(BF16), w2_ref[...])
    wv = -_softplus(-wl) - 0.5
    dec_o[...] = jnp.exp(-jnp.exp(wv))
    a = jax.nn.sigmoid(a0_ref[...] + _dot(_dot(xa, a1_ref[...]).astype(BF16), a2_ref[...]))
    g_o[...] = _dot(jax.nn.sigmoid(_dot(xg, g1_ref[...])).astype(BF16), g2_ref[...])
    k = k_ref[...]
    kk = k * kk_ref[...]
    kk = kk * lax.rsqrt(jnp.maximum(_group_sum(kk * kk, ones_ref), 1e-24))
    k2 = k * (1.0 + (a - 1.0) * ka_ref[...])
    nkk_o[...] = -kk
    b_o[...] = kk * a
    k2_o[...] = k2
    rkb_o[...] = _group_sum(r_ref[...] * k2 * rk_ref[...], ones_ref)


def rwkv_prep(h, xp, r, k, mu3, w1, w2, a1, a2, g1, g2, w0, a0, k_k, k_a, r_k, ones64):
    n, d = h.shape
    tm = _row_tile(n, 128)
    row = pl.BlockSpec((tm, d), lambda i: (i, 0))
    vec = pl.BlockSpec((1, d), lambda i: (0, 0))

    def full(a):
        return pl.BlockSpec(a.shape, lambda i: (0,) * a.ndim)

    mu3 = mu3.reshape(3, 1, d)
    vecs = [v.reshape(1, d) for v in (w0, a0, k_k, k_a, r_k)]
    return pl.pallas_call(
        _rwprep_kernel,
        grid=(n // tm,),
        in_specs=[row, row, row, row, full(mu3), full(w1), full(w2), full(a1), full(a2), full(g1), full(g2),
                  vec, vec, vec, vec, vec, full(ones64)],
        out_specs=[row] * 6,
        out_shape=[jax.ShapeDtypeStruct((n, d), F32)] * 6,
        compiler_params=_params("parallel"),
        name="rwkv_prep",
    )(h, xp, r, k, mu3, w1, w2, a1, a2, g1, g2, *vecs, ones64)


def _rwpost_kernel(y_ref, v_ref, rkb_ref, g_ref, x_ref, lnw_ref, lnb_ref, wo_ref, nw_ref, ones_ref,
                   x1_o, xn_o):
    y = y_ref[...]
    d = y - _group_sum(y, ones_ref) * (1.0 / RW_HEAD)
    var = _group_sum(d * d, ones_ref) * (1.0 / RW_HEAD)
    z = d * lax.rsqrt(var + RW_GN_EPS) * lnw_ref[...] + lnb_ref[...] + rkb_ref[...] * v_ref[...]
    x1 = x_ref[...] + _dot((z * g_ref[...]).astype(BF16), wo_ref[...])
    x1_o[...] = x1
    xn_o[...] = _rms(x1, nw_ref[...]).astype(xn_o.dtype)


def rwkv_post(y, v, rkb, g, x, ln_w, ln_b, w_o, norm_w, ones64):
    n, d = y.shape
    tm = _row_tile(n, 128)
    row = pl.BlockSpec((tm, d), lambda i: (i, 0))
    vec = pl.BlockSpec((1, d), lambda i: (0, 0))
    return pl.pallas_call(
        _rwpost_kernel,
        grid=(n // tm,),
        in_specs=[row, row, row, row, row, vec, vec, pl.BlockSpec((d, d), lambda i: (0, 0)), vec,
                  pl.BlockSpec((LANES, LANES), lambda i: (0, 0))],
        out_specs=[row, row],
        out_shape=[jax.ShapeDtypeStruct((n, d), F32), jax.ShapeDtypeStruct((n, d), BF16)],
        compiler_params=_params("parallel"),
        name="rwkv_post",
    )(y, v, rkb, g, x, ln_w.reshape(1, d), ln_b.reshape(1, d), w_o, norm_w.reshape(1, d), ones64)


def _hgin_kernel(kind, a_ref, w_ref, lb_ref, o_ref):
    o = _dot(a_ref[...], w_ref[...])
    if kind == "silu":
        o = o * jax.nn.sigmoid(o)
    elif kind == "forget":
        lb = lb_ref[...]
        o = lb + (1.0 - lb) * jax.nn.sigmoid(o)
    o_ref[...] = o


def hgrn_in(a, w, lb, kind):
    n, d = a.shape
    tm = _row_tile(n, 512)
    return pl.pallas_call(
        functools.partial(_hgin_kernel, kind),
        grid=(n // tm,),
        in_specs=[pl.BlockSpec((tm, d), lambda i: (i, 0)), pl.BlockSpec((d, d), lambda i: (0, 0)),
                  pl.BlockSpec((1, d), lambda i: (0, 0))],
        out_specs=pl.BlockSpec((tm, d), lambda i: (i, 0)),
        out_shape=jax.ShapeDtypeStruct((n, d), F32),
        compiler_params=_params("parallel"),
        name="hgrn_in_" + kind,
    )(a, w, lb.reshape(1, d))


def _hgpost_kernel(o_ref, sg_ref, x_ref, nw_ref, wo_ref, fw_ref, ones_ref, x3_o, xn_o):
    o = o_ref[...]
    ms = _group_sum(o * o, ones_ref) * (1.0 / HG_HEAD)
    z = o * lax.rsqrt(ms + RMS_EPS) * nw_ref[...] * sg_ref[...]
    x3 = x_ref[...] + _dot(z.astype(BF16), wo_ref[...])
    x3_o[...] = x3
    xn_o[...] = _rms(x3, fw_ref[...]).astype(xn_o.dtype)


def hgrn_post(o, sgate, x, norm_w_tiled, w_o, ffn_w, ones128):
    n, d = o.shape
    tm = _row_tile(n, 256)
    row = pl.BlockSpec((tm, d), lambda i: (i, 0))
    vec = pl.BlockSpec((1, d), lambda i: (0, 0))
    return pl.pallas_call(
        _hgpost_kernel,
        grid=(n // tm,),
        in_specs=[row, row, row, vec, pl.BlockSpec((d, d), lambda i: (0, 0)), vec,
                  pl.BlockSpec((LANES, LANES), lambda i: (0, 0))],
        out_specs=[row, row],
        out_shape=[jax.ShapeDtypeStruct((n, d), F32), jax.ShapeDtypeStruct((n, d), BF16)],
        compiler_params=_params("parallel"),
        name="hgrn_post",
    )(o, sgate, x, norm_w_tiled.reshape(1, d), w_o, ffn_w.reshape(1, d), ones128)


_NEG_BIG = -1e30


def _hgchunk_kernel(c, shared, hpg, f_ref, q_ref, v_ref, s0_ref, tril_ref, ones_ref, o_ref, st_ref, s_scr):
    tr = f_ref.shape[0]
    nseg = tr // c
    cp = max(c, 16)
    tile = pl.program_id(2) if shared else 0

    if shared:
        @pl.when(tile == 0)
        def _():
            s_scr[...] = s0_ref[...]
    else:
        s_scr[...] = s0_ref[...]

    row_in_chunk = lax.broadcasted_iota(jnp.int32, (tr, LANES), 0) & (c - 1)
    tril = tril_ref[...]

    def rows16(x):
        if cp == c:
            return x.astype(BF16)
        return jnp.concatenate([x, jnp.zeros((cp - c, x.shape[1]), F32)], axis=0).astype(BF16)

    for hh in range(hpg):
        sl = slice(hh * LANES, (hh + 1) * LANES)
        f, q, v = f_ref[:, sl], q_ref[:, sl], v_ref[:, sl]
        g = jnp.log(f)
        g_hi = g.astype(BF16)
        g_mid = (g - g_hi.astype(F32)).astype(BF16)
        g_lo = (g - g_hi.astype(F32) - g_mid.astype(F32)).astype(BF16)
        b = _dot(tril, g_hi) + _dot(tril, g_mid) + _dot(tril, g_lo)
        k = 1.0 - f
        terms = [(q * k).astype(BF16)]
        for d in range(1, c):
            arg = jnp.where(row_in_chunk >= d, b - pltpu.roll(b, d, 0), _NEG_BIG)
            terms.append((jnp.exp(arg) * q * pltpu.roll(k, d, 0)).astype(BF16))
        coef = _dot(jnp.concatenate(terms, axis=0), ones_ref[...])
        o = coef[0:tr] * v
        for d in range(1, c):
            o = o + coef[d * tr:(d + 1) * tr] * pltpu.roll(v, d, 0)
        qb = q * jnp.exp(b)
        inter = []
        for seg in range(nseg):
            slot = 0 if shared else seg
            r = slice(seg * c, (seg + 1) * c)
            s_old = s_scr[slot, hh]
            oi = lax.dot_general(rows16(qb[r]), s_old.astype(BF16), (((1,), (1,)), ((), ())),
                                 preferred_element_type=F32)
            inter.append(oi[:c])
            b_end = b[seg * c + c - 1:seg * c + c, :]
            kd = rows16(k[r] * jnp.exp(b_end - b[r]))
            upd = lax.dot_general(rows16(v[r]), kd, (((0,), (0,)), ((), ())), preferred_element_type=F32)
            s_scr[slot, hh] = s_old * jnp.exp(b_end) + upd
        o_ref[:, sl] = o + jnp.concatenate(inter, axis=0)

    if shared:
        @pl.when(tile == pl.num_programs(2) - 1)
        def _():
            st_ref[...] = s_scr[...]
    else:
        st_ref[...] = s_scr[...]


def hgrn_chunked(f, q, v, s0, row0, n_seq, t_len, c, tr, ones128):
    d = f.shape[1]
    hpg = 4
    ngrp = N_CHUNK // hpg
    shared = t_len >= tr
    i = jnp.arange(tr)
    tril = ((i[:, None] // c == i[None, :] // c) & (i[None, :] <= i[:, None])).astype(BF16)
    r0 = row0 // tr
    if shared:
        tiles = t_len // tr
        grid = (n_seq, ngrp, tiles)
        seq = pl.BlockSpec((tr, hpg * LANES), lambda s, g, t: (r0 + s * tiles + t, g))
        oseq = pl.BlockSpec((tr, hpg * LANES), lambda s, g, t: (s * tiles + t, g))
        state = pl.BlockSpec((1, hpg, HG_HEAD, LANES), lambda s, g, t: (s, g, 0, 0))
        const = lambda s, g, t: (0, 0)
        sem = ("parallel", "parallel", "arbitrary")
        nslot = 1
    else:
        nslot = tr // t_len
        grid = (n_seq // nslot, ngrp)
        seq = pl.BlockSpec((tr, hpg * LANES), lambda s, g: (r0 + s, g))
        oseq = pl.BlockSpec((tr, hpg * LANES), lambda s, g: (s, g))
        state = pl.BlockSpec((nslot, hpg, HG_HEAD, LANES), lambda s, g: (s, g, 0, 0))
        const = lambda s, g: (0, 0)
        sem = ("parallel", "parallel")
    return pl.pallas_call(
        functools.partial(_hgchunk_kernel, c, shared, hpg),
        grid=grid,
        in_specs=[seq, seq, seq, state, pl.BlockSpec((tr, tr), const), pl.BlockSpec((LANES, LANES), const)],
        out_specs=[oseq, state],
        out_shape=[jax.ShapeDtypeStruct((n_seq * t_len, d), F32), jax.ShapeDtypeStruct(s0.shape, F32)],
        scratch_shapes=[pltpu.VMEM((nslot, hpg, HG_HEAD, LANES), F32)],
        compiler_params=_params(*sem),
        name="hgrn_chunked",
    )(f, q, v, s0, tril, ones128)


def _scan_kernel(mode, bg, tb, *refs):
    rwkv = mode == "rwkv"
    rows = RW_HEAD if rwkv else HG_HEAD
    nc = bg * N_CHUNK
    sbl = min(rows, tb)
    nsb = tb // sbl
    if rwkv:
        (w_ref, nkk_ref, b_ref, k_ref, r_ref, v_ref, s0_ref, ones_ref,
         y_ref, st_ref, s_scr, vt_scr, yt_scr, l1, r1, l2, r2) = refs
    else:
        (w_ref, r_ref, v_ref, s0_ref, ones_ref,
         y_ref, st_ref, s_scr, vt_scr, yt_scr, l1, r1, l2, r2) = refs
    tblk = pl.program_id(1)
    lane = lax.broadcasted_iota(jnp.int32, (rows, LANES), 1)
    low = lane < RW_HEAD

    @pl.when(tblk == 0)
    def _():
        for c in range(nc):
            s_scr[c] = s0_ref[c // N_CHUNK, c % N_CHUNK]

    def chain(c):
        return c // N_CHUNK, slice((c % N_CHUNK) * LANES, (c % N_CHUNK + 1) * LANES)

    def row(ref, c, t):
        return ref[c // N_CHUNK, t, c % N_CHUNK:c % N_CHUNK + 1, :]

    def pair_slot(c):
        return slice((c // 2) * rows, (c // 2 + 1) * rows), slice((c % 2) * LANES, (c % 2 + 1) * LANES)

    for c in range(nc):
        bi, sl = chain(c)
        vb = v_ref[bi, :, sl]
        if tb < LANES:
            vb = jnp.concatenate([vb, jnp.zeros((LANES - tb, LANES), F32)], axis=0)
        vt = vb.T
        if rwkv:
            h0, h1 = vt[:RW_HEAD], vt[RW_HEAD:]
            vt_scr[c, 0] = jnp.where(low, h0, pltpu.roll(h1, RW_HEAD, 1))
            if nsb == 2:
                vt_scr[c, 1] = jnp.where(low, pltpu.roll(h0, RW_HEAD, 1), h1)
        else:
            vt_scr[c, 0] = vt
        for sb in range(nsb):
            yt_scr[c, sb] = jnp.zeros((rows, LANES), F32)

    def step(t, carry):
        sb = lax.shift_right_logical(t, int(math.log2(sbl))) if nsb > 1 else 0
        onehot = (lane & (rows - 1)) == (t & (sbl - 1))
        for c in range(nc):
            bi, sl = chain(c)
            vsel = jnp.where(onehot, vt_scr[c, sb], 0.0).astype(BF16)
            if rwkv:
                p = (s_scr[c] * row(nkk_ref, c, t)).astype(BF16)
                l1[c * rows:(c + 1) * rows, :] = jnp.concatenate([p, vsel], axis=1)
            else:
                rs, ls = pair_slot(c)
                l1[rs, ls] = vsel
        r1[...] = _dot(l1[...], ones_ref[...])
        for c in range(nc):
            bi, sl = chain(c)
            rs, ls = pair_slot(c)
            wrow = row(w_ref, c, t)
            if rwkv:
                sa = r1[c * rows:(c + 1) * rows, 0:LANES]
                vc = r1[c * rows:(c + 1) * rows, LANES:2 * LANES]
                s_new = (s_scr[c] * wrow + sa * row(b_ref, c, t)
                         + vc * row(k_ref, c, t))
            else:
                s_new = s_scr[c] * wrow + r1[rs, ls] * (1.0 - wrow)
            s_scr[c] = s_new
            l2[rs, ls] = (s_new * row(r_ref, c, t)).astype(BF16)
        r2[...] = _dot(l2[...], ones_ref[...])
        for c in range(nc):
            rs, ls = pair_slot(c)
            yt_scr[c, sb] = jnp.where(onehot, r2[rs, ls], yt_scr[c, sb])
        return carry

    lax.fori_loop(0, tb, step, 0)

    for c in range(nc):
        bi, sl = chain(c)
        if rwkv:
            y0 = yt_scr[c, 0]
            y1 = yt_scr[c, 1] if nsb == 2 else jnp.zeros((rows, LANES), F32)
            yt = jnp.concatenate([jnp.where(low, y0, pltpu.roll(y1, RW_HEAD, 1)),
                                  jnp.where(low, pltpu.roll(y0, RW_HEAD, 1), y1)], axis=0)
        else:
            yt = yt_scr[c, 0]
        y_ref[bi, :, sl] = yt.T[:tb]

    @pl.when(tblk == pl.num_programs(1) - 1)
    def _():
        for c in range(nc):
            st_ref[c // N_CHUNK, c % N_CHUNK] = s_scr[c]


def recurrence(mode, row_inputs, v, s0, ones, bg, tb):
    b, t, d = v.shape
    rows = s0.shape[2]
    nc = bg * N_CHUNK
    sbl = min(rows, tb)
    nsb = tb // sbl
    seq = pl.BlockSpec((bg, tb, d), lambda i, j: (i, j, 0))
    seq4 = pl.BlockSpec((bg, tb, N_CHUNK, LANES), lambda i, j: (i, j, 0, 0))
    state = pl.BlockSpec((bg, N_CHUNK, rows, LANES), lambda i, j: (i, 0, 0, 0))
    l1_rows = nc * rows if mode == "rwkv" else nc * rows // 2
    row_inputs = [a.reshape(b, t, N_CHUNK, LANES) for a in row_inputs] + [v]
    return pl.pallas_call(
        functools.partial(_scan_kernel, mode, bg, tb),
        grid=(b // bg, t // tb),
        in_specs=[seq4] * (len(row_inputs) - 1) + [seq, state,
                                                   pl.BlockSpec((2 * LANES, 2 * LANES), lambda i, j: (0, 0))],
        out_specs=[seq, state],
        out_shape=[jax.ShapeDtypeStruct((b, t, d), F32), jax.ShapeDtypeStruct(s0.shape, F32)],
        scratch_shapes=[pltpu.VMEM((nc, rows, LANES), F32),
                        pltpu.VMEM((nc, nsb, rows, LANES), F32),
                        pltpu.VMEM((nc, nsb, rows, LANES), F32),
                        pltpu.VMEM((l1_rows, 2 * LANES), BF16),
                        pltpu.VMEM((l1_rows, 2 * LANES), F32),
                        pltpu.VMEM((nc * rows // 2, 2 * LANES), BF16),
                        pltpu.VMEM((nc * rows // 2, 2 * LANES), F32)],
        compiler_params=_params("parallel", "arbitrary"),
        name="scan_" + mode,
    )(*row_inputs, s0, ones)


def _top_values(s, count):
    n = s.shape[0]
    rid = lax.broadcasted_iota(jnp.int32, s.shape, 0)
    vals = []
    for _ in range(count):
        m = jnp.max(s, axis=0, keepdims=True)
        vals.append(m)
        first = jnp.min(jnp.where(s == m, rid, n), axis=0, keepdims=True)
        s = jnp.where(rid == first, -jnp.inf, s)
    return vals


_CAND_PAIRS = [(a, b) for a in range(PK_TOPK) for b in range(PK_TOPK) if (a + 1) * (b + 1) <= PK_TOPK]
_CAND_ROWS = -(-len(_CAND_PAIRS) // 8) * 8


def _pquery_kernel(xn_ref, wq_ref, keys_ref, st_ref, thr_ref, cand_scr):
    q = _dot(xn_ref[...], wq_ref[...])
    for c in range(2 * PK_HEADS):
        qc = q[:, c * LANES:(c + 1) * LANES].astype(BF16)
        st_ref[c * LANES:(c + 1) * LANES, :] = lax.dot_general(
            keys_ref[c], qc, (((1,), (1,)), ((), ())), preferred_element_type=F32)

    def kth_largest_pair_sum(t1, t2):
        cand_scr[...] = jnp.full(cand_scr.shape, -jnp.inf, F32)
        for i, (a, b) in enumerate(_CAND_PAIRS):
            cand_scr[i:i + 1, :] = t1[a] + t2[b]
        return _top_values(cand_scr[...], PK_TOPK)

    for h in range(PK_HEADS):
        r1 = slice((2 * h) * LANES, (2 * h + 1) * LANES)
        r2 = slice((2 * h + 1) * LANES, (2 * h + 2) * LANES)
        s1, s2 = st_ref[r1, :], st_ref[r2, :]
        t1 = _top_values(s1, PK_TOPK)
        t2 = _top_values(s2, PK_TOPK)
        top = kth_largest_pair_sum(t1, t2)
        z = jnp.zeros_like(top[0])
        for m in top:
            z = z + jnp.exp(m - top[0])
        off = top[0] + jnp.log(z) + LN2
        st_ref[r1, :] = (s1 - off) * LOG2E
        st_ref[r2, :] = s2 * LOG2E
        top = kth_largest_pair_sum([(t - off) * LOG2E for t in t1], [t * LOG2E for t in t2])
        thr_ref[h:h + 1, :] = top[-1]


def peer_query(xn, w_q, keys):
    n, d = xn.shape
    tm = _row_tile(n, 256)
    return pl.pallas_call(
        _pquery_kernel,
        grid=(n // tm,),
        in_specs=[pl.BlockSpec((tm, d), lambda i: (i, 0)), pl.BlockSpec((d, d), lambda i: (0, 0)),
                  pl.BlockSpec(keys.shape, lambda i: (0, 0, 0))],
        out_specs=[pl.BlockSpec((d, tm), lambda i: (0, i)), pl.BlockSpec((PK_HEADS, tm), lambda i: (0, i))],
        out_shape=[jax.ShapeDtypeStruct((d, n), F32), jax.ShapeDtypeStruct((PK_HEADS, n), F32)],
        scratch_shapes=[pltpu.VMEM((_CAND_ROWS, tm), F32)],
        compiler_params=_params("parallel"),
        name="peer_query",
    )(xn, w_q, keys)


def _pdense_kernel(ec, xn_ref, u_ref, vt_ref, st_ref, thr_ref, y_ref, acc, hid_scr):
    j = pl.program_id(1)
    groups = ec // PK_NKEYS
    half = groups // 2

    @pl.when(j == 0)
    def _():
        acc[...] = jnp.zeros(acc.shape, F32)

    hid_scr[j % 2] = lax.dot_general(u_ref[...], xn_ref[...], (((1,), (1,)), ((), ())),
                                     preferred_element_type=F32)
    live = j > 0
    jb = jnp.maximum(j - 1, 0)
    prev = 1 - j % 2
    for part in range(2):
        ps = []
        for ii in range(part * half, (part + 1) * half):
            gate = None
            for h in range(PK_HEADS):
                c = (st_ref[pl.ds(2 * h * LANES + jb * groups + ii, 1), :]
                     + st_ref[(2 * h + 1) * LANES:(2 * h + 2) * LANES, :])
                e = jnp.where(c >= thr_ref[h:h + 1, :], jnp.exp2(c), 0.0)
                gate = e if gate is None else gate + e
            hh = hid_scr[prev, ii * PK_NKEYS:(ii + 1) * PK_NKEYS, :]
            p = hh * (1.0 + lax.erf(hh * INV_SQRT2)) * gate
            ps.append(jnp.where(live, p, 0.0).astype(BF16))
        acc[...] += _dot(vt_ref[:, part * half * PK_NKEYS:(part + 1) * half * PK_NKEYS],
                         jnp.concatenate(ps, axis=0))

    @pl.when(j == pl.num_programs(1) - 1)
    def _():
        y_ref[...] = acc[...].T


def peer_dense(xn, u, v_t, st, thr):
    n, d = xn.shape
    tm = _row_tile(n, 512)
    ec = 1024
    nchunk = PK_EXPERTS // ec
    return pl.pallas_call(
        functools.partial(_pdense_kernel, ec),
        grid=(n // tm, nchunk + 1),
        in_specs=[pl.BlockSpec((tm, d), lambda i, j: (i, 0)),
                  pl.BlockSpec((ec, d), lambda i, j: (jnp.minimum(j, nchunk - 1), 0)),
                  pl.BlockSpec((d, ec), lambda i, j: (0, jnp.maximum(j - 1, 0))),
                  pl.BlockSpec((d, tm), lambda i, j: (0, i)),
                  pl.BlockSpec((PK_HEADS, tm), lambda i, j: (0, i))],
        out_specs=pl.BlockSpec((tm, d), lambda i, j: (i, 0)),
        out_shape=jax.ShapeDtypeStruct((n, d), F32),
        scratch_shapes=[pltpu.VMEM((d, tm), F32), pltpu.VMEM((2, ec, tm), F32)],
        compiler_params=_params("parallel", "arbitrary"),
        name="peer_dense",
    )(xn, u, v_t, st, thr)


def peer(xn, w_q, keys, u, v_t):
    st, thr = peer_query(xn, w_q, keys)
    return peer_dense(xn, u, v_t, st, thr)


def _pack_rwkv_state(s):
    b = s.shape[0]
    return s.reshape(b, N_CHUNK, 2, RW_HEAD, RW_HEAD).transpose(0, 1, 3, 2, 4).reshape(b, N_CHUNK, RW_HEAD, LANES)


def _unpack_rwkv_state(s):
    b = s.shape[0]
    return s.reshape(b, N_CHUNK, RW_HEAD, 2, RW_HEAD).transpose(0, 1, 3, 2, 4).reshape(b, 2 * N_CHUNK, RW_HEAD, RW_HEAD)


def _shifted(h, shift0):
    return jnp.concatenate([shift0[:, None].astype(h.dtype), h[:, :-1]], axis=1)


def kernel(x_prompt, x_sample, state_rwkv_wkv, state_rwkv_shift, state_hgrn, ln_mix_w, ln_ffn_w, ln_f_w, rw_mu, rw_w_rkv, rw_w0, rw_w1, rw_w2, rw_a0, rw_a1, rw_a2, rw_g1, rw_g2, rw_k_k, rw_k_a, rw_r_k, rw_ln_w, rw_ln_b, rw_w_o, hg_w_in, hg_lb, hg_norm_w, hg_w_o, pk_w_q, pk_keys, pk_u, pk_v):
    bp, tp, d = x_prompt.shape
    bs, ts, _ = x_sample.shape
    n_p, n_s = bp * tp, bs * ts
    ones64 = _block_ones(RW_HEAD)
    ones128 = _block_ones(HG_HEAD)
    ones64x4 = _block_ones(RW_HEAD, 2 * LANES)

    def split(a):
        return a[:n_p].reshape(bp, tp, d), a[n_p:].reshape(bs, ts, d)

    def both(fn, a_list, s0_p, s0_s):
        parts = [split(a) for a in a_list]
        yp, sp = fn([p[0] for p in parts], s0_p, True)
        ys, ss = fn([p[1] for p in parts], s0_s, False)
        return jnp.concatenate([yp.reshape(n_p, d), ys.reshape(n_s, d)], axis=0), sp, ss

    x = jnp.concatenate([x_prompt.reshape(n_p, d), x_sample.reshape(n_s, d)], axis=0)

    h = rmsnorm(x, ln_mix_w[0])
    h_p, h_s = split(h)
    xp = jnp.concatenate([_shifted(h_p, jnp.zeros((bp, d), F32)).reshape(n_p, d),
                          _shifted(h_s, state_rwkv_shift[0]).reshape(n_s, d)], axis=0)
    rkv = rwkv_rkv(h, xp, rw_mu[0, :3], rw_w_rkv[0].astype(BF16))
    lora_pad = LANES - rw_w1.shape[-1]
    pad_in = lambda w: jnp.pad(w, ((0, 0), (0, lora_pad))).astype(BF16)
    pad_out = lambda w: jnp.pad(w, ((0, lora_pad), (0, 0))).astype(BF16)
    dec, nkk, bb, k2, g, rkb = rwkv_prep(
        h, xp, rkv[0], rkv[1], rw_mu[0, 3:], pad_in(rw_w1[0]), pad_out(rw_w2[0]), pad_in(rw_a1[0]),
        pad_out(rw_a2[0]), rw_g1[0].astype(BF16), rw_g2[0].astype(BF16), rw_w0[0], rw_a0[0], rw_k_k[0],
        rw_k_a[0], rw_r_k[0].reshape(d), ones64)

    def rw_scan(rows_in, s0, prompt):
        return recurrence("rwkv", rows_in[:-1], rows_in[-1], s0, ones64x4, 2 if prompt else 4,
                          min(LANES, rows_in[0].shape[1]))

    y, wkv_p, wkv_s = both(rw_scan, [dec, nkk, bb, k2, rkv[0], rkv[2]],
                           jnp.zeros((bp, N_CHUNK, RW_HEAD, LANES), F32), _pack_rwkv_state(state_rwkv_wkv[0]))
    x, xn = rwkv_post(y, rkv[2], rkb, g, x, rw_ln_w[0], rw_ln_b[0], rw_w_o[0].astype(BF16), ln_ffn_w[0], ones64)
    yp = peer(xn, pk_w_q[0].astype(BF16), pk_keys[0].reshape(2 * PK_HEADS, PK_NKEYS, LANES).astype(BF16),
              pk_u[0].astype(BF16), pk_v[0].astype(BF16).T)
    x, h1 = add_norm(x, yp, ln_mix_w[1], BF16)

    lb = jax.nn.softmax(hg_lb.astype(F32), axis=0)
    lb = (jnp.cumsum(lb, axis=0) - lb[0])[1]
    w_in = hg_w_in[0].astype(BF16)
    q = hgrn_in(h1, w_in[:, :d], lb, "silu")
    f = hgrn_in(h1, w_in[:, d:2 * d], lb, "forget")
    vin = hgrn_in(h1, w_in[:, 2 * d:3 * d], lb, "none")
    sgate = hgrn_in(h1, w_in[:, 3 * d:], lb, "silu")

    o_p, hg_p = hgrn_chunked(f, q, vin, jnp.zeros((bp, N_CHUNK, HG_HEAD, LANES), F32), 0, bp, tp, 16, LANES,
                             ones128)
    o_s, hg_s = hgrn_chunked(f, q, vin, jnp.swapaxes(state_hgrn[0], -1, -2), n_p, bs, ts, ts, 8 * ts, ones128)
    o = jnp.concatenate([o_p, o_s], axis=0)
    x, xn = hgrn_post(o, sgate, x, jnp.tile(hg_norm_w[0], N_CHUNK), hg_w_o[0].astype(BF16), ln_ffn_w[1], ones128)
    yp = peer(xn, pk_w_q[1].astype(BF16), pk_keys[1].reshape(2 * PK_HEADS, PK_NKEYS, LANES).astype(BF16),
              pk_u[1].astype(BF16), pk_v[1].astype(BF16).T)
    _, out = add_norm(x, yp, ln_f_w, F32)

    y_p, y_s = split(out)
    return (y_p, y_s,
            _unpack_rwkv_state(wkv_p)[None], h_p[:, -1][None], jnp.swapaxes(hg_p, -1, -2)[None],
            _unpack_rwkv_state(wkv_s)[None], h_s[:, -1][None], jnp.swapaxes(hg_s, -1, -2)[None])
```

```python
import functools
import math

import jax
import jax.numpy as jnp
from jax import lax
from jax.experimental import pallas as pl
from jax.experimental.pallas import tpu as pltpu

F32 = jnp.float32
BF16 = jnp.bfloat16

D_MODEL = 2048
LANES = 128
N_CHUNK = D_MODEL // LANES
RW_HEAD = 64
HG_HEAD = 128
RW_GN_EPS = 64e-5
RMS_EPS = 1e-6
PK_HEADS = 8
PK_NKEYS = 128
PK_TOPK = 16
PK_EXPERTS = PK_NKEYS * PK_NKEYS
INV_SQRT2 = 1.0 / math.sqrt(2.0)
LN2 = math.log(2.0)
LOG2E = 1.0 / LN2
VMEM_LIMIT = 56 * 1024 * 1024


def _params(*sem, flags=None):
    return pltpu.CompilerParams(dimension_semantics=sem, vmem_limit_bytes=VMEM_LIMIT, flags=flags)


def _row_tile(n, cap):
    t = cap
    while n % t:
        t //= 2
    return t


def _dot(a, b):
    return jnp.dot(a, b, preferred_element_type=F32)


def _rms(x, w):
    return x * lax.rsqrt(jnp.mean(x * x, axis=-1, keepdims=True) + RMS_EPS) * w


def _group_sum(x, ones_ref):
    tm, d = x.shape
    nch = d // LANES
    xs = jnp.concatenate([x[:, j * LANES:(j + 1) * LANES] for j in range(nch)], axis=0)
    hi = xs.astype(BF16)
    lo = (xs - hi.astype(F32)).astype(BF16)
    w = ones_ref[...]
    r = _dot(hi, w) + _dot(lo, w)
    return jnp.concatenate([r[j * tm:(j + 1) * tm] for j in range(nch)], axis=1)


def _block_ones(group, size=LANES):
    i = jnp.arange(size) // group
    return (i[:, None] == i[None, :]).astype(BF16)


def _rms_kernel(x_ref, w_ref, o_ref):
    o_ref[...] = _rms(x_ref[...], w_ref[...]).astype(o_ref.dtype)


def rmsnorm(x, w, out_dtype=F32):
    n, d = x.shape
    tm = _row_tile(n, 512)
    return pl.pallas_call(
        _rms_kernel,
        grid=(n // tm,),
        in_specs=[pl.BlockSpec((tm, d), lambda i: (i, 0)), pl.BlockSpec((1, d), lambda i: (0, 0))],
        out_specs=pl.BlockSpec((tm, d), lambda i: (i, 0)),
        out_shape=jax.ShapeDtypeStruct((n, d), out_dtype),
        compiler_params=_params("parallel"),
        name="rmsnorm",
    )(x, w.reshape(1, d))


def _addnorm_kernel(x_ref, y_ref, w_ref, s_ref, n_ref):
    s = x_ref[...] + y_ref[...]
    s_ref[...] = s
    n_ref[...] = _rms(s, w_ref[...]).astype(n_ref.dtype)


def add_norm(x, y, w, norm_dtype):
    n, d = x.shape
    tm = _row_tile(n, 512)
    row = pl.BlockSpec((tm, d), lambda i: (i, 0))
    return pl.pallas_call(
        _addnorm_kernel,
        grid=(n // tm,),
        in_specs=[row, row, pl.BlockSpec((1, d), lambda i: (0, 0))],
        out_specs=[row, row],
        out_shape=[jax.ShapeDtypeStruct((n, d), F32), jax.ShapeDtypeStruct((n, d), norm_dtype)],
        compiler_params=_params("parallel"),
        name="add_norm",
    )(x, y, w.reshape(1, d))


def _rkv_kernel(h_ref, xp_ref, mu_ref, w_ref, o_ref):
    h = h_ref[...]
    xm = h + (xp_ref[...] - h) * mu_ref[0]
    o_ref[0] = _dot(xm.astype(BF16), w_ref[0])


def rwkv_rkv(h, xp, mu3, w3):
    n, d = h.shape
    tm = _row_tile(n, 512)
    row = pl.BlockSpec((tm, d), lambda s, i: (i, 0))
    return pl.pallas_call(
        _rkv_kernel,
        grid=(3, n // tm),
        in_specs=[row, row,
                  pl.BlockSpec((1, 1, d), lambda s, i: (s, 0, 0)),
                  pl.BlockSpec((1, d, d), lambda s, i: (s, 0, 0))],
        out_specs=pl.BlockSpec((1, tm, d), lambda s, i: (s, i, 0)),
        out_shape=jax.ShapeDtypeStruct((3, n, d), F32),
        compiler_params=_params("arbitrary", "arbitrary"),
        name="rwkv_rkv",
    )(h, xp, mu3.reshape(3, 1, d), w3)


def _softplus(z):
    return jnp.maximum(z, 0.0) + jnp.log1p(jnp.exp(-jnp.abs(z)))


def _rwprep_kernel(h_ref, xp_ref, r_ref, k_ref, mu_ref, w1_ref, w2_ref, a1_ref, a2_ref, g1_ref, g2_ref,
                   w0_ref, a0_ref, kk_ref, ka_ref, rk_ref, ones_ref,
                   dec_o, nkk_o, b_o, k2_o, g_o, rkb_o):
    h = h_ref[...]
    dx = xp_ref[...] - h
    xw = (h + dx * mu_ref[0]).astype(BF16)
    xa = (h + dx * mu_ref[1]).astype(BF16)
    xg = (h + dx * mu_ref[2]).astype(BF16)
    wl = w0_ref[...] + _dot(jnp.tanh(_dot(xw, w1_ref[...])).astype(BF16), w2_ref[...])
    wv = -_softplus(-wl) - 0.5
    dec_o[...] = jnp.exp(-jnp.exp(wv))
    a = jax.nn.sigmoid(a0_ref[...] + _dot(_dot(xa, a1_ref[...]).astype(BF16), a2_ref[...]))
    g_o[...] = _dot(jax.nn.sigmoid(_dot(xg, g1_ref[...])).astype(BF16), g2_ref[...])
    k = k_ref[...]
    kk = k * kk_ref[...]
    kk = kk * lax.rsqrt(jnp.maximum(_group_sum(kk * kk, ones_ref), 1e-24))
    k2 = k * (1.0 + (a - 1.0) * ka_ref[...])
    nkk_o[...] = -kk
    b_o[...] = kk * a
    k2_o[...] = k2
    rkb_o[...] = _group_sum(r_ref[...] * k2 * rk_ref[...], ones_ref)


def rwkv_prep(h, xp, r, k, mu3, w1, w2, a1, a2, g1, g2, w0, a0, k_k, k_a, r_k, ones64):
    n, d = h.shape
    tm = _row_tile(n, 128)
    row = pl.BlockSpec((tm, d), lambda i: (i, 0))
    vec = pl.BlockSpec((1, d), lambda i: (0, 0))

    def full(a):
        return pl.BlockSpec(a.shape, lambda i: (0,) * a.ndim)

    mu3 = mu3.reshape(3, 1, d)
    vecs = [v.reshape(1, d) for v in (w0, a0, k_k, k_a, r_k)]
    return pl.pallas_call(
        _rwprep_kernel,
        grid=(n // tm,),
        in_specs=[row, row, row, row, full(mu3), full(w1), full(w2), full(a1), full(a2), full(g1), full(g2),
                  vec, vec, vec, vec, vec, full(ones64)],
        out_specs=[row] * 6,
        out_shape=[jax.ShapeDtypeStruct((n, d), F32)] * 6,
        compiler_params=_params("parallel"),
        name="rwkv_prep",
    )(h, xp, r, k, mu3, w1, w2, a1, a2, g1, g2, *vecs, ones64)


def _rwpost_kernel(y_ref, v_ref, rkb_ref, g_ref, x_ref, lnw_ref, lnb_ref, wo_ref, nw_ref, ones_ref,
                   x1_o, xn_o):
    y = y_ref[...]
    d = y - _group_sum(y, ones_ref) * (1.0 / RW_HEAD)
    var = _group_sum(d * d, ones_ref) * (1.0 / RW_HEAD)
    z = d * lax.rsqrt(var + RW_GN_EPS) * lnw_ref[...] + lnb_ref[...] + rkb_ref[...] * v_ref[...]
    x1 = x_ref[...] + _dot((z * g_ref[...]).astype(BF16), wo_ref[...])
    x1_o[...] = x1
    xn_o[...] = _rms(x1, nw_ref[...]).astype(xn_o.dtype)


def rwkv_post(y, v, rkb, g, x, ln_w, ln_b, w_o, norm_w, ones64):
    n, d = y.shape
    tm = _row_tile(n, 128)
    row = pl.BlockSpec((tm, d), lambda i: (i, 0))
    vec = pl.BlockSpec((1, d), lambda i: (0, 0))
    return pl.pallas_call(
        _rwpost_kernel,
        grid=(n // tm,),
        in_specs=[row, row, row, row, row, vec, vec, pl.BlockSpec((d, d), lambda i: (0, 0)), vec,
                  pl.BlockSpec((LANES, LANES), lambda i: (0, 0))],
        out_specs=[row, row],
        out_shape=[jax.ShapeDtypeStruct((n, d), F32), jax.ShapeDtypeStruct((n, d), BF16)],
        compiler_params=_params("parallel"),
        name="rwkv_post",
    )(y, v, rkb, g, x, ln_w.reshape(1, d), ln_b.reshape(1, d), w_o, norm_w.reshape(1, d), ones64)


def _hgin_kernel(kind, a_ref, w_ref, lb_ref, o_ref):
    o = _dot(a_ref[...], w_ref[...])
    if kind == "silu":
        o = o * jax.nn.sigmoid(o)
    elif kind == "forget":
        lb = lb_ref[...]
        o = lb + (1.0 - lb) * jax.nn.sigmoid(o)
    o_ref[...] = o


def hgrn_in(a, w, lb, kind):
    n, d = a.shape
    tm = _row_tile(n, 512)
    return pl.pallas_call(
        functools.partial(_hgin_kernel, kind),
        grid=(n // tm,),
        in_specs=[pl.BlockSpec((tm, d), lambda i: (i, 0)), pl.BlockSpec((d, d), lambda i: (0, 0)),
                  pl.BlockSpec((1, d), lambda i: (0, 0))],
        out_specs=pl.BlockSpec((tm, d), lambda i: (i, 0)),
        out_shape=jax.ShapeDtypeStruct((n, d), F32),
        compiler_params=_params("parallel"),
        name="hgrn_in_" + kind,
    )(a, w, lb.reshape(1, d))


def _hgpost_kernel(o_ref, sg_ref, x_ref, nw_ref, wo_ref, fw_ref, ones_ref, x3_o, xn_o):
    o = o_ref[...]
    ms = _group_sum(o * o, ones_ref) * (1.0 / HG_HEAD)
    z = o * lax.rsqrt(ms + RMS_EPS) * nw_ref[...] * sg_ref[...]
    x3 = x_ref[...] + _dot(z.astype(BF16), wo_ref[...])
    x3_o[...] = x3
    xn_o[...] = _rms(x3, fw_ref[...]).astype(xn_o.dtype)


def hgrn_post(o, sgate, x, norm_w_tiled, w_o, ffn_w, ones128):
    n, d = o.shape
    tm = _row_tile(n, 256)
    row = pl.BlockSpec((tm, d), lambda i: (i, 0))
    vec = pl.BlockSpec((1, d), lambda i: (0, 0))
    return pl.pallas_call(
        _hgpost_kernel,
        grid=(n // tm,),
        in_specs=[row, row, row, vec, pl.BlockSpec((d, d), lambda i: (0, 0)), vec,
                  pl.BlockSpec((LANES, LANES), lambda i: (0, 0))],
        out_specs=[row, row],
        out_shape=[jax.ShapeDtypeStruct((n, d), F32), jax.ShapeDtypeStruct((n, d), BF16)],
        compiler_params=_params("parallel"),
        name="hgrn_post",
    )(o, sgate, x, norm_w_tiled.reshape(1, d), w_o, ffn_w.reshape(1, d), ones128)


_NEG_BIG = -1e30


def _hgchunk_kernel(c, shared, hpg, f_ref, q_ref, v_ref, s0_ref, tril_ref, ones_ref, o_ref, st_ref, s_scr):
    tr = f_ref.shape[0]
    nseg = tr // c
    cp = max(c, 16)
    tile = pl.program_id(2) if shared else 0

    if shared:
        @pl.when(tile == 0)
        def _():
            s_scr[...] = s0_ref[...]
    else:
        s_scr[...] = s0_ref[...]

    row_in_chunk = lax.broadcasted_iota(jnp.int32, (tr, LANES), 0) & (c - 1)
    tril = tril_ref[...]

    def rows16(x):
        if cp == c:
            return x.astype(BF16)
        return jnp.concatenate([x, jnp.zeros((cp - c, x.shape[1]), F32)], axis=0).astype(BF16)

    for hh in range(hpg):
        sl = slice(hh * LANES, (hh + 1) * LANES)
        f, q, v = f_ref[:, sl], q_ref[:, sl], v_ref[:, sl]
        g = jnp.log(f)
        g_hi = g.astype(BF16)
        g_mid = (g - g_hi.astype(F32)).astype(BF16)
        g_lo = (g - g_hi.astype(F32) - g_mid.astype(F32)).astype(BF16)
        b = _dot(tril, g_hi) + _dot(tril, g_mid) + _dot(tril, g_lo)
        k = 1.0 - f
        terms = [(q * k).astype(BF16)]
        for d in range(1, c):
            arg = jnp.where(row_in_chunk >= d, b - pltpu.roll(b, d, 0), _NEG_BIG)
            terms.append((jnp.exp(arg) * q * pltpu.roll(k, d, 0)).astype(BF16))
        coef = _dot(jnp.concatenate(terms, axis=0), ones_ref[...])
        o = coef[0:tr] * v
        for d in range(1, c):
            o = o + coef[d * tr:(d + 1) * tr] * pltpu.roll(v, d, 0)
        qb = q * jnp.exp(b)
        inter = []
        for seg in range(nseg):
            slot = 0 if shared else seg
            r = slice(seg * c, (seg + 1) * c)
            s_old = s_scr[slot, hh]
            oi = lax.dot_general(rows16(qb[r]), s_old.astype(BF16), (((1,), (1,)), ((), ())),
                                 preferred_element_type=F32)
            inter.append(oi[:c])
            b_end = b[seg * c + c - 1:seg * c + c, :]
            kd = rows16(k[r] * jnp.exp(b_end - b[r]))
            upd = lax.dot_general(rows16(v[r]), kd, (((0,), (0,)), ((), ())), preferred_element_type=F32)
            s_scr[slot, hh] = s_old * jnp.exp(b_end) + upd
        o_ref[:, sl] = o + jnp.concatenate(inter, axis=0)

    if shared:
        @pl.when(tile == pl.num_programs(2) - 1)
        def _():
            st_ref[...] = s_scr[...]
    else:
        st_ref[...] = s_scr[...]


def hgrn_chunked(f, q, v, s0, row0, n_seq, t_len, c, tr, ones128):
    d = f.shape[1]
    hpg = 4
    ngrp = N_CHUNK // hpg
    shared = t_len >= tr
    i = jnp.arange(tr)
    tril = ((i[:, None] // c == i[None, :] // c) & (i[None, :] <= i[:, None])).astype(BF16)
    r0 = row0 // tr
    if shared:
        tiles = t_len // tr
        grid = (n_seq, ngrp, tiles)
        seq = pl.BlockSpec((tr, hpg * LANES), lambda s, g, t: (r0 + s * tiles + t, g))
        oseq = pl.BlockSpec((tr, hpg * LANES), lambda s, g, t: (s * tiles + t, g))
        state = pl.BlockSpec((1, hpg, HG_HEAD, LANES), lambda s, g, t: (s, g, 0, 0))
        const = lambda s, g, t: (0, 0)
        sem = ("parallel", "parallel", "arbitrary")
        nslot = 1
    else:
        nslot = tr // t_len
        grid = (n_seq // nslot, ngrp)
        seq = pl.BlockSpec((tr, hpg * LANES), lambda s, g: (r0 + s, g))
        oseq = pl.BlockSpec((tr, hpg * LANES), lambda s, g: (s, g))
        state = pl.BlockSpec((nslot, hpg, HG_HEAD, LANES), lambda s, g: (s, g, 0, 0))
        const = lambda s, g: (0, 0)
        sem = ("parallel", "parallel")
    return pl.pallas_call(
        functools.partial(_hgchunk_kernel, c, shared, hpg),
        grid=grid,
        in_specs=[seq, seq, seq, state, pl.BlockSpec((tr, tr), const), pl.BlockSpec((LANES, LANES), const)],
        out_specs=[oseq, state],
        out_shape=[jax.ShapeDtypeStruct((n_seq * t_len, d), F32), jax.ShapeDtypeStruct(s0.shape, F32)],
        scratch_shapes=[pltpu.VMEM((nslot, hpg, HG_HEAD, LANES), F32)],
        compiler_params=_params(*sem),
        name="hgrn_chunked",
    )(f, q, v, s0, tril, ones128)


def _scan_kernel(mode, bg, tb, *refs):
    rwkv = mode == "rwkv"
    rows = RW_HEAD if rwkv else HG_HEAD
    nc = bg * N_CHUNK
    sbl = min(rows, tb)
    nsb = tb // sbl
    if rwkv:
        (w_ref, nkk_ref, b_ref, k_ref, r_ref, v_ref, s0_ref, ones_ref,
         y_ref, st_ref, s_scr, vt_scr, yt_scr, l1, r1, l2, r2) = refs
    else:
        (w_ref, r_ref, v_ref, s0_ref, ones_ref,
         y_ref, st_ref, s_scr, vt_scr, yt_scr, l1, r1, l2, r2) = refs
    tblk = pl.program_id(1)
    lane = lax.broadcasted_iota(jnp.int32, (rows, LANES), 1)
    low = lane < RW_HEAD

    @pl.when(tblk == 0)
    def _():
        for c in range(nc):
            s_scr[c] = s0_ref[c // N_CHUNK, c % N_CHUNK]

    def chain(c):
        return c // N_CHUNK, slice((c % N_CHUNK) * LANES, (c % N_CHUNK + 1) * LANES)

    def row(ref, c, t):
        return ref[c // N_CHUNK, t, c % N_CHUNK:c % N_CHUNK + 1, :]

    def pair_slot(c):
        return slice((c // 2) * rows, (c // 2 + 1) * rows), slice((c % 2) * LANES, (c % 2 + 1) * LANES)

    for c in range(nc):
        bi, sl = chain(c)
        vb = v_ref[bi, :, sl]
        if tb < LANES:
            vb = jnp.concatenate([vb, jnp.zeros((LANES - tb, LANES), F32)], axis=0)
        vt = vb.T
        if rwkv:
            h0, h1 = vt[:RW_HEAD], vt[RW_HEAD:]
            vt_scr[c, 0] = jnp.where(low, h0, pltpu.roll(h1, RW_HEAD, 1))
            if nsb == 2:
                vt_scr[c, 1] = jnp.where(low, pltpu.roll(h0, RW_HEAD, 1), h1)
        else:
            vt_scr[c, 0] = vt
        for sb in range(nsb):
            yt_scr[c, sb] = jnp.zeros((rows, LANES), F32)

    def step(t, carry):
        sb = lax.shift_right_logical(t, int(math.log2(sbl))) if nsb > 1 else 0
        onehot = (lane & (rows - 1)) == (t & (sbl - 1))
        for c in range(nc):
            bi, sl = chain(c)
            vsel = jnp.where(onehot, vt_scr[c, sb], 0.0).astype(BF16)
            if rwkv:
                p = (s_scr[c] * row(nkk_ref, c, t)).astype(BF16)
                l1[c * rows:(c + 1) * rows, :] = jnp.concatenate([p, vsel], axis=1)
            else:
                rs, ls = pair_slot(c)
                l1[rs, ls] = vsel
        r1[...] = _dot(l1[...], ones_ref[...])
        for c in range(nc):
            bi, sl = chain(c)
            rs, ls = pair_slot(c)
            wrow = row(w_ref, c, t)
            if rwkv:
                sa = r1[c * rows:(c + 1) * rows, 0:LANES]
                vc = r1[c * rows:(c + 1) * rows, LANES:2 * LANES]
                s_new = (s_scr[c] * wrow + sa * row(b_ref, c, t)
                         + vc * row(k_ref, c, t))
            else:
                s_new = s_scr[c] * wrow + r1[rs, ls] * (1.0 - wrow)
            s_scr[c] = s_new
            l2[rs, ls] = (s_new * row(r_ref, c, t)).astype(BF16)
        r2[...] = _dot(l2[...], ones_ref[...])
        for c in range(nc):
            rs, ls = pair_slot(c)
            yt_scr[c, sb] = jnp.where(onehot, r2[rs, ls], yt_scr[c, sb])
        return carry

    lax.fori_loop(0, tb, step, 0)

    for c in range(nc):
        bi, sl = chain(c)
        if rwkv:
            y0 = yt_scr[c, 0]
            y1 = yt_scr[c, 1] if nsb == 2 else jnp.zeros((rows, LANES), F32)
            yt = jnp.concatenate([jnp.where(low, y0, pltpu.roll(y1, RW_HEAD, 1)),
                                  jnp.where(low, pltpu.roll(y0, RW_HEAD, 1), y1)], axis=0)
        else:
            yt = yt_scr[c, 0]
        y_ref[bi, :, sl] = yt.T[:tb]

    @pl.when(tblk == pl.num_programs(1) - 1)
    def _():
        for c in range(nc):
            st_ref[c // N_CHUNK, c % N_CHUNK] = s_scr[c]


def recurrence(mode, row_inputs, v, s0, ones, bg, tb):
    b, t, d = v.shape
    rows = s0.shape[2]
    nc = bg * N_CHUNK
    sbl = min(rows, tb)
    nsb = tb // sbl
    seq = pl.BlockSpec((bg, tb, d), lambda i, j: (i, j, 0))
    seq4 = pl.BlockSpec((bg, tb, N_CHUNK, LANES), lambda i, j: (i, j, 0, 0))
    state = pl.BlockSpec((bg, N_CHUNK, rows, LANES), lambda i, j: (i, 0, 0, 0))
    l1_rows = nc * rows if mode == "rwkv" else nc * rows // 2
    row_inputs = [a.reshape(b, t, N_CHUNK, LANES) for a in row_inputs] + [v]
    return pl.pallas_call(
        functools.partial(_scan_kernel, mode, bg, tb),
        grid=(b // bg, t // tb),
        in_specs=[seq4] * (len(row_inputs) - 1) + [seq, state,
                                                   pl.BlockSpec((2 * LANES, 2 * LANES), lambda i, j: (0, 0))],
        out_specs=[seq, state],
        out_shape=[jax.ShapeDtypeStruct((b, t, d), F32), jax.ShapeDtypeStruct(s0.shape, F32)],
        scratch_shapes=[pltpu.VMEM((nc, rows, LANES), F32),
                        pltpu.VMEM((nc, nsb, rows, LANES), F32),
                        pltpu.VMEM((nc, nsb, rows, LANES), F32),
                        pltpu.VMEM((l1_rows, 2 * LANES), BF16),
                        pltpu.VMEM((l1_rows, 2 * LANES), F32),
                        pltpu.VMEM((nc * rows // 2, 2 * LANES), BF16),
                        pltpu.VMEM((nc * rows // 2, 2 * LANES), F32)],
        compiler_params=_params("parallel", "arbitrary"),
        name="scan_" + mode,
    )(*row_inputs, s0, ones)


def _top_values(s, count):
    n = s.shape[0]
    rid = lax.broadcasted_iota(jnp.int32, s.shape, 0)
    vals = []
    for _ in range(count):
        m = jnp.max(s, axis=0, keepdims=True)
        vals.append(m)
        first = jnp.min(jnp.where(s == m, rid, n), axis=0, keepdims=True)
        s = jnp.where(rid == first, -jnp.inf, s)
    return vals


_CAND_PAIRS = [(a, b) for a in range(PK_TOPK) for b in range(PK_TOPK) if (a + 1) * (b + 1) <= PK_TOPK]
_CAND_ROWS = -(-len(_CAND_PAIRS) // 8) * 8


def _pquery_kernel(xn_ref, wq_ref, keys_ref, st_ref, thr_ref, cand_scr):
    q = _dot(xn_ref[...], wq_ref[...])
    for c in range(2 * PK_HEADS):
        qc = q[:, c * LANES:(c + 1) * LANES].astype(BF16)
        st_ref[c * LANES:(c + 1) * LANES, :] = lax.dot_general(
            keys_ref[c], qc, (((1,), (1,)), ((), ())), preferred_element_type=F32)

    def kth_largest_pair_sum(t1, t2):
        cand_scr[...] = jnp.full(cand_scr.shape, -jnp.inf, F32)
        for i, (a, b) in enumerate(_CAND_PAIRS):
            cand_scr[i:i + 1, :] = t1[a] + t2[b]
        return _top_values(cand_scr[...], PK_TOPK)

    for h in range(PK_HEADS):
        r1 = slice((2 * h) * LANES, (2 * h + 1) * LANES)
        r2 = slice((2 * h + 1) * LANES, (2 * h + 2) * LANES)
        s1, s2 = st_ref[r1, :], st_ref[r2, :]
        t1 = _top_values(s1, PK_TOPK)
        t2 = _top_values(s2, PK_TOPK)
        top = kth_largest_pair_sum(t1, t2)
        z = jnp.zeros_like(top[0])
        for m in top:
            z = z + jnp.exp(m - top[0])
        off = top[0] + jnp.log(z) + LN2
        st_ref[r1, :] = (s1 - off) * LOG2E
        st_ref[r2, :] = s2 * LOG2E
        top = kth_largest_pair_sum([(t - off) * LOG2E for t in t1], [t * LOG2E for t in t2])
        thr_ref[h:h + 1, :] = top[-1]


def peer_query(xn, w_q, keys):
    n, d = xn.shape
    tm = _row_tile(n, 256)
    return pl.pallas_call(
        _pquery_kernel,
        grid=(n // tm,),
        in_specs=[pl.BlockSpec((tm, d), lambda i: (i, 0)), pl.BlockSpec((d, d), lambda i: (0, 0)),
                  pl.BlockSpec(keys.shape, lambda i: (0, 0, 0))],
        out_specs=[pl.BlockSpec((d, tm), lambda i: (0, i)), pl.BlockSpec((PK_HEADS, tm), lambda i: (0, i))],
        out_shape=[jax.ShapeDtypeStruct((d, n), F32), jax.ShapeDtypeStruct((PK_HEADS, n), F32)],
        scratch_shapes=[pltpu.VMEM((_CAND_ROWS, tm), F32)],
        compiler_params=_params("parallel"),
        name="peer_query",
    )(xn, w_q, keys)


PD_PARTS = 4


def _pdense_kernel(ec, xn_ref, u_ref, vt_ref, st_ref, thr_ref, y_ref, acc, xt_scr, hid_scr):
    j = pl.program_id(1)
    groups = ec // PK_NKEYS
    sub = 8

    @pl.when(j == 0)
    def _():
        acc[...] = jnp.zeros(acc.shape, F32)
        xt_scr[...] = xn_ref[...].astype(F32).T.astype(BF16)

    tm = xn_ref.shape[0]
    gpp = groups // PD_PARTS
    rows = gpp * PK_NKEYS
    thr = [jnp.broadcast_to(thr_ref[h:h + 1, :], (sub, tm)) for h in range(PK_HEADS)]

    def hidden(part):
        r = slice(part * rows, (part + 1) * rows)
        hid_scr[r, :] = _dot(u_ref[r, :], xt_scr[...])

    def finish(part):
        ps = []
        for ii in range(part * gpp, (part + 1) * gpp):
            gates = [None] * (PK_NKEYS // sub)
            for h in range(PK_HEADS):
                s1 = jnp.broadcast_to(st_ref[pl.ds(2 * h * LANES + j * groups + ii, 1), :], (sub, tm))
                for s in range(PK_NKEYS // sub):
                    c = s1 + st_ref[(2 * h + 1) * LANES + s * sub:(2 * h + 1) * LANES + (s + 1) * sub, :]
                    e = jnp.where(c >= thr[h], jnp.exp2(c), 0.0)
                    gates[s] = e if gates[s] is None else gates[s] + e
            hh = hid_scr[ii * PK_NKEYS:(ii + 1) * PK_NKEYS, :]
            ps.append((hh * (1.0 + lax.erf(hh * INV_SQRT2)) * jnp.concatenate(gates, axis=0)).astype(BF16))
        acc[...] += _dot(vt_ref[:, part * rows:(part + 1) * rows], jnp.concatenate(ps, axis=0))

    hidden(0)
    for part in range(PD_PARTS):
        if part + 1 < PD_PARTS:
            hidden(part + 1)
        finish(part)

    @pl.when(j == pl.num_programs(1) - 1)
    def _():
        y_ref[...] = acc[...].T


def peer_dense(xn, u, v_t, st, thr):
    n, d = xn.shape
    tm = _row_tile(n, 512)
    ec = 1024
    return pl.pallas_call(
        functools.partial(_pdense_kernel, ec),
        grid=(n // tm, PK_EXPERTS // ec),
        in_specs=[pl.BlockSpec((tm, d), lambda i, j: (i, 0)),
                  pl.BlockSpec((ec, d), lambda i, j: (j, 0)),
                  pl.BlockSpec((d, ec), lambda i, j: (0, j)),
                  pl.BlockSpec((d, tm), lambda i, j: (0, i)),
                  pl.BlockSpec((PK_HEADS, tm), lambda i, j: (0, i))],
        out_specs=pl.BlockSpec((tm, d), lambda i, j: (i, 0)),
        out_shape=jax.ShapeDtypeStruct((n, d), F32),
        scratch_shapes=[pltpu.VMEM((d, tm), F32), pltpu.VMEM((d, tm), BF16), pltpu.VMEM((ec, tm), F32)],
        compiler_params=_params("parallel", "arbitrary"),
        name="peer_dense",
    )(xn, u, v_t, st, thr)


def peer(xn, w_q, keys, u, v_t):
    st, thr = peer_query(xn, w_q, keys)
    return peer_dense(xn, u, v_t, st, thr)


def _pack_rwkv_state(s):
    b = s.shape[0]
    return s.reshape(b, N_CHUNK, 2, RW_HEAD, RW_HEAD).transpose(0, 1, 3, 2, 4).reshape(b, N_CHUNK, RW_HEAD, LANES)


def _unpack_rwkv_state(s):
    b = s.shape[0]
    return s.reshape(b, N_CHUNK, RW_HEAD, 2, RW_HEAD).transpose(0, 1, 3, 2, 4).reshape(b, 2 * N_CHUNK, RW_HEAD, RW_HEAD)


def _shifted(h, shift0):
    return jnp.concatenate([shift0[:, None].astype(h.dtype), h[:, :-1]], axis=1)


def kernel(x_prompt, x_sample, state_rwkv_wkv, state_rwkv_shift, state_hgrn, ln_mix_w, ln_ffn_w, ln_f_w, rw_mu, rw_w_rkv, rw_w0, rw_w1, rw_w2, rw_a0, rw_a1, rw_a2, rw_g1, rw_g2, rw_k_k, rw_k_a, rw_r_k, rw_ln_w, rw_ln_b, rw_w_o, hg_w_in, hg_lb, hg_norm_w, hg_w_o, pk_w_q, pk_keys, pk_u, pk_v):
    bp, tp, d = x_prompt.shape
    bs, ts, _ = x_sample.shape
    n_p, n_s = bp * tp, bs * ts
    ones64 = _block_ones(RW_HEAD)
    ones128 = _block_ones(HG_HEAD)
    ones64x4 = _block_ones(RW_HEAD, 2 * LANES)

    def split(a):
        return a[:n_p].reshape(bp, tp, d), a[n_p:].reshape(bs, ts, d)

    def both(fn, a_list, s0_p, s0_s):
        parts = [split(a) for a in a_list]
        yp, sp = fn([p[0] for p in parts], s0_p, True)
        ys, ss = fn([p[1] for p in parts], s0_s, False)
        return jnp.concatenate([yp.reshape(n_p, d), ys.reshape(n_s, d)], axis=0), sp, ss

    x = jnp.concatenate([x_prompt.reshape(n_p, d), x_sample.reshape(n_s, d)], axis=0)

    h = rmsnorm(x, ln_mix_w[0])
    h_p, h_s = split(h)
    xp = jnp.concatenate([_shifted(h_p, jnp.zeros((bp, d), F32)).reshape(n_p, d),
                          _shifted(h_s, state_rwkv_shift[0]).reshape(n_s, d)], axis=0)
    rkv = rwkv_rkv(h, xp, rw_mu[0, :3], rw_w_rkv[0].astype(BF16))
    lora_pad = LANES - rw_w1.shape[-1]
    pad_in = lambda w: jnp.pad(w, ((0, 0), (0, lora_pad))).astype(BF16)
    pad_out = lambda w: jnp.pad(w, ((0, lora_pad), (0, 0))).astype(BF16)
    dec, nkk, bb, k2, g, rkb = rwkv_prep(
        h, xp, rkv[0], rkv[1], rw_mu[0, 3:], pad_in(rw_w1[0]), pad_out(rw_w2[0]), pad_in(rw_a1[0]),
        pad_out(rw_a2[0]), rw_g1[0].astype(BF16), rw_g2[0].astype(BF16), rw_w0[0], rw_a0[0], rw_k_k[0],
        rw_k_a[0], rw_r_k[0].reshape(d), ones64)

    def rw_scan(rows_in, s0, prompt):
        return recurrence("rwkv", rows_in[:-1], rows_in[-1], s0, ones64x4, 4, min(RW_HEAD, rows_in[0].shape[1]))

    y, wkv_p, wkv_s = both(rw_scan, [dec, nkk, bb, k2, rkv[0], rkv[2]],
                           jnp.zeros((bp, N_CHUNK, RW_HEAD, LANES), F32), _pack_rwkv_state(state_rwkv_wkv[0]))
    x, xn = rwkv_post(y, rkv[2], rkb, g, x, rw_ln_w[0], rw_ln_b[0], rw_w_o[0].astype(BF16), ln_ffn_w[0], ones64)
    yp = peer(xn, pk_w_q[0].astype(BF16), pk_keys[0].reshape(2 * PK_HEADS, PK_NKEYS, LANES).astype(BF16),
              pk_u[0].astype(BF16), pk_v[0].astype(BF16).T)
    x, h1 = add_norm(x, yp, ln_mix_w[1], BF16)

    lb = jax.nn.softmax(hg_lb.astype(F32), axis=0)
    lb = (jnp.cumsum(lb, axis=0) - lb[0])[1]
    w_in = hg_w_in[0].astype(BF16)
    q = hgrn_in(h1, w_in[:, :d], lb, "silu")
    f = hgrn_in(h1, w_in[:, d:2 * d], lb, "forget")
    vin = hgrn_in(h1, w_in[:, 2 * d:3 * d], lb, "none")
    sgate = hgrn_in(h1, w_in[:, 3 * d:], lb, "silu")

    o_p, hg_p = hgrn_chunked(f, q, vin, jnp.zeros((bp, N_CHUNK, HG_HEAD, LANES), F32), 0, bp, tp, 16, LANES,
                             ones128)
    o_s, hg_s = hgrn_chunked(f, q, vin, jnp.swapaxes(state_hgrn[0], -1, -2), n_p, bs, ts, ts, 8 * ts, ones128)
    o = jnp.concatenate([o_p, o_s], axis=0)
    x, xn = hgrn_post(o, sgate, x, jnp.tile(hg_norm_w[0], N_CHUNK), hg_w_o[0].astype(BF16), ln_ffn_w[1], ones128)
    yp = peer(xn, pk_w_q[1].astype(BF16), pk_keys[1].reshape(2 * PK_HEADS, PK_NKEYS, LANES).astype(BF16),
              pk_u[1].astype(BF16), pk_v[1].astype(BF16).T)
    _, out = add_norm(x, yp, ln_f_w, F32)

    y_p, y_s = split(out)
    return (y_p, y_s,
            _unpack_rwkv_state(wkv_p)[None], h_p[:, -1][None], jnp.swapaxes(hg_p, -1, -2)[None],
            _unpack_rwkv_state(wkv_s)[None], h_s[:, -1][None], jnp.swapaxes(hg_s, -1, -2)[None])
```

```python
import functools
import math

import jax
import jax.numpy as jnp
from jax import lax
from jax.experimental import pallas as pl
from jax.experimental.pallas import tpu as pltpu

F32 = jnp.float32
BF16 = jnp.bfloat16

D_MODEL = 2048
LANES = 128
N_CHUNK = D_MODEL // LANES
RW_HEAD = 64
HG_HEAD = 128
RW_GN_EPS = 64e-5
RMS_EPS = 1e-6
PK_HEADS = 8
PK_NKEYS = 128
PK_TOPK = 16
PK_EXPERTS = PK_NKEYS * PK_NKEYS
INV_SQRT2 = 1.0 / math.sqrt(2.0)
LN2 = math.log(2.0)
LOG2E = 1.0 / LN2
VMEM_LIMIT = 56 * 1024 * 1024


def _params(*sem, flags=None):
    return pltpu.CompilerParams(dimension_semantics=sem, vmem_limit_bytes=VMEM_LIMIT, flags=flags)


def _row_tile(n, cap):
    t = cap
    while n % t:
        t //= 2
    return t


def _dot(a, b):
    return jnp.dot(a, b, preferred_element_type=F32)


def _rms(x, w):
    return x * lax.rsqrt(jnp.mean(x * x, axis=-1, keepdims=True) + RMS_EPS) * w


def _group_sum(x, ones_ref):
    tm, d = x.shape
    nch = d // LANES
    xs = jnp.concatenate([x[:, j * LANES:(j + 1) * LANES] for j in range(nch)], axis=0)
    hi = xs.astype(BF16)
    lo = (xs - hi.astype(F32)).astype(BF16)
    w = ones_ref[...]
    r = _dot(hi, w) + _dot(lo, w)
    return jnp.concatenate([r[j * tm:(j + 1) * tm] for j in range(nch)], axis=1)


def _block_ones(group, size=LANES):
    i = jnp.arange(size) // group
    return (i[:, None] == i[None, :]).astype(BF16)


def _rms_kernel(x_ref, w_ref, o_ref):
    o_ref[...] = _rms(x_ref[...], w_ref[...]).astype(o_ref.dtype)


def rmsnorm(x, w, out_dtype=F32):
    n, d = x.shape
    tm = _row_tile(n, 512)
    return pl.pallas_call(
        _rms_kernel,
        grid=(n // tm,),
        in_specs=[pl.BlockSpec((tm, d), lambda i: (i, 0)), pl.BlockSpec((1, d), lambda i: (0, 0))],
        out_specs=pl.BlockSpec((tm, d), lambda i: (i, 0)),
        out_shape=jax.ShapeDtypeStruct((n, d), out_dtype),
        compiler_params=_params("parallel"),
        name="rmsnorm",
    )(x, w.reshape(1, d))


def _addnorm_kernel(x_ref, y_ref, w_ref, s_ref, n_ref):
    s = x_ref[...] + y_ref[...]
    s_ref[...] = s
    n_ref[...] = _rms(s, w_ref[...]).astype(n_ref.dtype)


def add_norm(x, y, w, norm_dtype):
    n, d = x.shape
    tm = _row_tile(n, 512)
    row = pl.BlockSpec((tm, d), lambda i: (i, 0))
    return pl.pallas_call(
        _addnorm_kernel,
        grid=(n // tm,),
        in_specs=[row, row, pl.BlockSpec((1, d), lambda i: (0, 0))],
        out_specs=[row, row],
        out_shape=[jax.ShapeDtypeStruct((n, d), F32), jax.ShapeDtypeStruct((n, d), norm_dtype)],
        compiler_params=_params("parallel"),
        name="add_norm",
    )(x, y, w.reshape(1, d))


def _rkv_kernel(h_ref, xp_ref, mu_ref, w_ref, o_ref):
    h = h_ref[...]
    xm = h + (xp_ref[...] - h) * mu_ref[0]
    o_ref[0] = _dot(xm.astype(BF16), w_ref[0])


def rwkv_rkv(h, xp, mu3, w3):
    n, d = h.shape
    tm = _row_tile(n, 512)
    row = pl.BlockSpec((tm, d), lambda s, i: (i, 0))
    return pl.pallas_call(
        _rkv_kernel,
        grid=(3, n // tm),
        in_specs=[row, row,
                  pl.BlockSpec((1, 1, d), lambda s, i: (s, 0, 0)),
                  pl.BlockSpec((1, d, d), lambda s, i: (s, 0, 0))],
        out_specs=pl.BlockSpec((1, tm, d), lambda s, i: (s, i, 0)),
        out_shape=jax.ShapeDtypeStruct((3, n, d), F32),
        compiler_params=_params("arbitrary", "arbitrary"),
        name="rwkv_rkv",
    )(h, xp, mu3.reshape(3, 1, d), w3)


def _softplus(z):
    return jnp.maximum(z, 0.0) + jnp.log1p(jnp.exp(-jnp.abs(z)))


def _rwprep_kernel(h_ref, xp_ref, r_ref, k_ref, mu_ref, w1_ref, w2_ref, a1_ref, a2_ref, g1_ref, g2_ref,
                   w0_ref, a0_ref, kk_ref, ka_ref, rk_ref, ones_ref,
                   dec_o, nkk_o, b_o, k2_o, g_o, rkb_o):
    h = h_ref[...]
    dx = xp_ref[...] - h
    xw = (h + dx * mu_ref[0]).astype(BF16)
    xa = (h + dx * mu_ref[1]).astype(BF16)
    xg = (h + dx * mu_ref[2]).astype(BF16)
    wl = w0_ref[...] + _dot(jnp.tanh(_dot(xw, w1_ref[...])).astype(BF16), w2_ref[...])
    wv = -_softplus(-wl) - 0.5
    dec_o[...] = jnp.exp(-jnp.exp(wv))
    a = jax.nn.sigmoid(a0_ref[...] + _dot(_dot(xa, a1_ref[...]).astype(BF16), a2_ref[...]))
    g_o[...] = _dot(jax.nn.sigmoid(_dot(xg, g1_ref[...])).astype(BF16), g2_ref[...])
    k = k_ref[...]
    kk = k * kk_ref[...]
    kk = kk * lax.rsqrt(jnp.maximum(_group_sum(kk * kk, ones_ref), 1e-24))
    k2 = k * (1.0 + (a - 1.0) * ka_ref[...])
    nkk_o[...] = -kk
    b_o[...] = kk * a
    k2_o[...] = k2
    rkb_o[...] = _group_sum(r_ref[...] * k2 * rk_ref[...], ones_ref)


def rwkv_prep(h, xp, r, k, mu3, w1, w2, a1, a2, g1, g2, w0, a0, k_k, k_a, r_k, ones64):
    n, d = h.shape
    tm = _row_tile(n, 128)
    row = pl.BlockSpec((tm, d), lambda i: (i, 0))
    vec = pl.BlockSpec((1, d), lambda i: (0, 0))

    def full(a):
        return pl.BlockSpec(a.shape, lambda i: (0,) * a.ndim)

    mu3 = mu3.reshape(3, 1, d)
    vecs = [v.reshape(1, d) for v in (w0, a0, k_k, k_a, r_k)]
    return pl.pallas_call(
        _rwprep_kernel,
        grid=(n // tm,),
        in_specs=[row, row, row, row, full(mu3), full(w1), full(w2), full(a1), full(a2), full(g1), full(g2),
                  vec, vec, vec, vec, vec, full(ones64)],
        out_specs=[row] * 6,
        out_shape=[jax.ShapeDtypeStruct((n, d), F32)] * 6,
        compiler_params=_params("parallel"),
        name="rwkv_prep",
    )(h, xp, r, k, mu3, w1, w2, a1, a2, g1, g2, *vecs, ones64)


def _rwpost_kernel(y_ref, v_ref, rkb_ref, g_ref, x_ref, lnw_ref, lnb_ref, wo_ref, nw_ref, ones_ref,
                   x1_o, xn_o):
    y = y_ref[...]
    d = y - _group_sum(y, ones_ref) * (1.0 / RW_HEAD)
    var = _group_sum(d * d, ones_ref) * (1.0 / RW_HEAD)
    z = d * lax.rsqrt(var + RW_GN_EPS) * lnw_ref[...] + lnb_ref[...] + rkb_ref[...] * v_ref[...]
    x1 = x_ref[...] + _dot((z * g_ref[...]).astype(BF16), wo_ref[...])
    x1_o[...] = x1
    xn_o[...] = _rms(x1, nw_ref[...]).astype(xn_o.dtype)


def rwkv_post(y, v, rkb, g, x, ln_w, ln_b, w_o, norm_w, ones64):
    n, d = y.shape
    tm = _row_tile(n, 128)
    row = pl.BlockSpec((tm, d), lambda i: (i, 0))
    vec = pl.BlockSpec((1, d), lambda i: (0, 0))
    return pl.pallas_call(
        _rwpost_kernel,
        grid=(n // tm,),
        in_specs=[row, row, row, row, row, vec, vec, pl.BlockSpec((d, d), lambda i: (0, 0)), vec,
                  pl.BlockSpec((LANES, LANES), lambda i: (0, 0))],
        out_specs=[row, row],
        out_shape=[jax.ShapeDtypeStruct((n, d), F32), jax.ShapeDtypeStruct((n, d), BF16)],
        compiler_params=_params("parallel"),
        name="rwkv_post",
    )(y, v, rkb, g, x, ln_w.reshape(1, d), ln_b.reshape(1, d), w_o, norm_w.reshape(1, d), ones64)


def _hgin_kernel(kind, a_ref, w_ref, lb_ref, o_ref):
    o = _dot(a_ref[...], w_ref[...])
    if kind == "silu":
        o = o * jax.nn.sigmoid(o)
    elif kind == "forget":
        lb = lb_ref[...]
        o = lb + (1.0 - lb) * jax.nn.sigmoid(o)
    o_ref[...] = o


def hgrn_in(a, w, lb, kind):
    n, d = a.shape
    tm = _row_tile(n, 512)
    return pl.pallas_call(
        functools.partial(_hgin_kernel, kind),
        grid=(n // tm,),
        in_specs=[pl.BlockSpec((tm, d), lambda i: (i, 0)), pl.BlockSpec((d, d), lambda i: (0, 0)),
                  pl.BlockSpec((1, d), lambda i: (0, 0))],
        out_specs=pl.BlockSpec((tm, d), lambda i: (i, 0)),
        out_shape=jax.ShapeDtypeStruct((n, d), F32),
        compiler_params=_params("parallel"),
        name="hgrn_in_" + kind,
    )(a, w, lb.reshape(1, d))


def _hgpost_kernel(o_ref, sg_ref, x_ref, nw_ref, wo_ref, fw_ref, ones_ref, x3_o, xn_o):
    o = o_ref[...]
    ms = _group_sum(o * o, ones_ref) * (1.0 / HG_HEAD)
    z = o * lax.rsqrt(ms + RMS_EPS) * nw_ref[...] * sg_ref[...]
    x3 = x_ref[...] + _dot(z.astype(BF16), wo_ref[...])
    x3_o[...] = x3
    xn_o[...] = _rms(x3, fw_ref[...]).astype(xn_o.dtype)


def hgrn_post(o, sgate, x, norm_w_tiled, w_o, ffn_w, ones128):
    n, d = o.shape
    tm = _row_tile(n, 256)
    row = pl.BlockSpec((tm, d), lambda i: (i, 0))
    vec = pl.BlockSpec((1, d), lambda i: (0, 0))
    return pl.pallas_call(
        _hgpost_kernel,
        grid=(n // tm,),
        in_specs=[row, row, row, vec, pl.BlockSpec((d, d), lambda i: (0, 0)), vec,
                  pl.BlockSpec((LANES, LANES), lambda i: (0, 0))],
        out_specs=[row, row],
        out_shape=[jax.ShapeDtypeStruct((n, d), F32), jax.ShapeDtypeStruct((n, d), BF16)],
        compiler_params=_params("parallel"),
        name="hgrn_post",
    )(o, sgate, x, norm_w_tiled.reshape(1, d), w_o, ffn_w.reshape(1, d), ones128)


_NEG_BIG = -1e30


def _hgchunk_kernel(c, shared, hpg, f_ref, q_ref, v_ref, s0_ref, tril_ref, ones_ref, o_ref, st_ref, s_scr):
    tr = f_ref.shape[0]
    nseg = tr // c
    cp = max(c, 16)
    tile = pl.program_id(2) if shared else 0

    if shared:
        @pl.when(tile == 0)
        def _():
            s_scr[...] = s0_ref[...]
    else:
        s_scr[...] = s0_ref[...]

    row_in_chunk = lax.broadcasted_iota(jnp.int32, (tr, LANES), 0) & (c - 1)
    tril = tril_ref[...]

    def rows16(x):
        if cp == c:
            return x.astype(BF16)
        return jnp.concatenate([x, jnp.zeros((cp - c, x.shape[1]), F32)], axis=0).astype(BF16)

    for hh in range(hpg):
        sl = slice(hh * LANES, (hh + 1) * LANES)
        f, q, v = f_ref[:, sl], q_ref[:, sl], v_ref[:, sl]
        g = jnp.log(f)
        g_hi = g.astype(BF16)
        g_mid = (g - g_hi.astype(F32)).astype(BF16)
        g_lo = (g - g_hi.astype(F32) - g_mid.astype(F32)).astype(BF16)
        b = _dot(tril, g_hi) + _dot(tril, g_mid) + _dot(tril, g_lo)
        k = 1.0 - f
        terms = [(q * k).astype(BF16)]
        for d in range(1, c):
            arg = jnp.where(row_in_chunk >= d, b - pltpu.roll(b, d, 0), _NEG_BIG)
            terms.append((jnp.exp(arg) * q * pltpu.roll(k, d, 0)).astype(BF16))
        coef = _dot(jnp.concatenate(terms, axis=0), ones_ref[...])
        o = coef[0:tr] * v
        for d in range(1, c):
            o = o + coef[d * tr:(d + 1) * tr] * pltpu.roll(v, d, 0)
        qb = q * jnp.exp(b)
        decays, updates = [], []
        for seg in range(nseg):
            r = slice(seg * c, (seg + 1) * c)
            b_end = b[seg * c + c - 1:seg * c + c, :]
            kd = rows16(k[r] * jnp.exp(b_end - b[r]))
            decays.append(jnp.exp(b_end))
            updates.append(lax.dot_general(rows16(v[r]), kd, (((0,), (0,)), ((), ())),
                                           preferred_element_type=F32))
        starts = []
        for seg in range(nseg):
            slot = 0 if shared else seg
            s_old = s_scr[slot, hh]
            starts.append(s_old.astype(BF16))
            s_scr[slot, hh] = s_old * decays[seg] + updates[seg]
        inter = []
        for seg in range(nseg):
            oi = lax.dot_general(rows16(qb[seg * c:(seg + 1) * c]), starts[seg], (((1,), (1,)), ((), ())),
                                 preferred_element_type=F32)
            inter.append(oi[:c])
        o_ref[:, sl] = o + jnp.concatenate(inter, axis=0)

    if shared:
        @pl.when(tile == pl.num_programs(2) - 1)
        def _():
            st_ref[...] = s_scr[...]
    else:
        st_ref[...] = s_scr[...]


def hgrn_chunked(f, q, v, s0, row0, n_seq, t_len, c, tr, ones128):
    d = f.shape[1]
    hpg = 4
    ngrp = N_CHUNK // hpg
    shared = t_len >= tr
    i = jnp.arange(tr)
    tril = ((i[:, None] // c == i[None, :] // c) & (i[None, :] <= i[:, None])).astype(BF16)
    r0 = row0 // tr
    if shared:
        tiles = t_len // tr
        grid = (n_seq, ngrp, tiles)
        seq = pl.BlockSpec((tr, hpg * LANES), lambda s, g, t: (r0 + s * tiles + t, g))
        oseq = pl.BlockSpec((tr, hpg * LANES), lambda s, g, t: (s * tiles + t, g))
        state = pl.BlockSpec((1, hpg, HG_HEAD, LANES), lambda s, g, t: (s, g, 0, 0))
        const = lambda s, g, t: (0, 0)
        sem = ("parallel", "parallel", "arbitrary")
        nslot = 1
    else:
        nslot = tr // t_len
        grid = (n_seq // nslot, ngrp)
        seq = pl.BlockSpec((tr, hpg * LANES), lambda s, g: (r0 + s, g))
        oseq = pl.BlockSpec((tr, hpg * LANES), lambda s, g: (s, g))
        state = pl.BlockSpec((nslot, hpg, HG_HEAD, LANES), lambda s, g: (s, g, 0, 0))
        const = lambda s, g: (0, 0)
        sem = ("parallel", "parallel")
    return pl.pallas_call(
        functools.partial(_hgchunk_kernel, c, shared, hpg),
        grid=grid,
        in_specs=[seq, seq, seq, state, pl.BlockSpec((tr, tr), const), pl.BlockSpec((LANES, LANES), const)],
        out_specs=[oseq, state],
        out_shape=[jax.ShapeDtypeStruct((n_seq * t_len, d), F32), jax.ShapeDtypeStruct(s0.shape, F32)],
        scratch_shapes=[pltpu.VMEM((nslot, hpg, HG_HEAD, LANES), F32)],
        compiler_params=_params(*sem),
        name="hgrn_chunked",
    )(f, q, v, s0, tril, ones128)


SCAN_GROUPS = 2


def _scan_kernel(bg, tb, w_ref, nkk_ref, b_ref, k_ref, r_ref, v_ref, s0_ref, ones_ref,
                 y_ref, st_ref, s_scr, vt_scr, yt_scr, l1, r1, l2, r2):
    rows = RW_HEAD
    nc = bg * N_CHUNK
    ncg = nc // SCAN_GROUPS
    tblk = pl.program_id(1)
    lane = lax.broadcasted_iota(jnp.int32, (rows, LANES), 1)
    low = lane < RW_HEAD

    @pl.when(tblk == 0)
    def _():
        for c in range(nc):
            s_scr[c] = s0_ref[c // N_CHUNK, c % N_CHUNK]

    def lanes_of(c):
        return slice((c % N_CHUNK) * LANES, (c % N_CHUNK + 1) * LANES)

    def row(ref, c, t):
        return ref[c // N_CHUNK, t, c % N_CHUNK:c % N_CHUNK + 1, :]

    def pair_slot(cl):
        return slice((cl // 2) * rows, (cl // 2 + 1) * rows), slice((cl % 2) * LANES, (cl % 2 + 1) * LANES)

    for c in range(nc):
        vb = v_ref[c // N_CHUNK, :, lanes_of(c)]
        vb = jnp.concatenate([vb, jnp.zeros((LANES - tb, LANES), F32)], axis=0)
        vt = vb.T
        vt_scr[c] = jnp.where(low, vt[:RW_HEAD], pltpu.roll(vt[RW_HEAD:], RW_HEAD, 1))
        yt_scr[c] = jnp.zeros((rows, LANES), F32)

    def reduce_state(g, t, onehot):
        for cl in range(ncg):
            c = g * ncg + cl
            p = (s_scr[c] * row(nkk_ref, c, t)).astype(BF16)
            vsel = jnp.where(onehot, vt_scr[c], 0.0).astype(BF16)
            l1[g, cl * rows:(cl + 1) * rows, :] = jnp.concatenate([p, vsel], axis=1)
        r1[g] = _dot(l1[g], ones_ref[...])

    def update(g, t):
        for cl in range(ncg):
            c = g * ncg + cl
            rs, ls = pair_slot(cl)
            sa = r1[g, cl * rows:(cl + 1) * rows, 0:LANES]
            vc = r1[g, cl * rows:(cl + 1) * rows, LANES:2 * LANES]
            s_new = s_scr[c] * row(w_ref, c, t) + sa * row(b_ref, c, t) + vc * row(k_ref, c, t)
            s_scr[c] = s_new
            l2[g, rs, ls] = (s_new * row(r_ref, c, t)).astype(BF16)
        r2[g] = _dot(l2[g], ones_ref[...])

    def collect(g, onehot):
        for cl in range(ncg):
            rs, ls = pair_slot(cl)
            c = g * ncg + cl
            yt_scr[c] = jnp.where(onehot, r2[g, rs, ls], yt_scr[c])

    def step(t, carry):
        onehot = (lane & (RW_HEAD - 1)) == t
        for g in range(SCAN_GROUPS):
            reduce_state(g, t, onehot)
        for g in range(SCAN_GROUPS):
            update(g, t)
        for g in range(SCAN_GROUPS):
            collect(g, onehot)
        return carry

    lax.fori_loop(0, tb, step, 0, unroll=2)

    for c in range(nc):
        y0 = yt_scr[c]
        yt = jnp.concatenate([jnp.where(low, y0, 0.0), jnp.where(low, pltpu.roll(y0, RW_HEAD, 1), 0.0)], axis=0)
        y_ref[c // N_CHUNK, :, lanes_of(c)] = yt.T[:tb]

    @pl.when(tblk == pl.num_programs(1) - 1)
    def _():
        for c in range(nc):
            st_ref[c // N_CHUNK, c % N_CHUNK] = s_scr[c]


def rwkv_recurrence(row_inputs, v, s0, ones, bg, tb):
    b, t, d = v.shape
    assert tb <= RW_HEAD and t % tb == 0 and b % bg == 0
    nc = bg * N_CHUNK
    ncg = nc // SCAN_GROUPS
    seq = pl.BlockSpec((bg, tb, d), lambda i, j: (i, j, 0))
    seq4 = pl.BlockSpec((bg, tb, N_CHUNK, LANES), lambda i, j: (i, j, 0, 0))
    state = pl.BlockSpec((bg, N_CHUNK, RW_HEAD, LANES), lambda i, j: (i, 0, 0, 0))
    row_inputs = [a.reshape(b, t, N_CHUNK, LANES) for a in row_inputs]
    return pl.pallas_call(
        functools.partial(_scan_kernel, bg, tb),
        grid=(b // bg, t // tb),
        in_specs=[seq4] * len(row_inputs) + [seq, state, pl.BlockSpec((2 * LANES, 2 * LANES), lambda i, j: (0, 0))],
        out_specs=[seq, state],
        out_shape=[jax.ShapeDtypeStruct((b, t, d), F32), jax.ShapeDtypeStruct(s0.shape, F32)],
        scratch_shapes=[pltpu.VMEM((nc, RW_HEAD, LANES), F32),
                        pltpu.VMEM((nc, RW_HEAD, LANES), F32),
                        pltpu.VMEM((nc, RW_HEAD, LANES), F32),
                        pltpu.VMEM((SCAN_GROUPS, ncg * RW_HEAD, 2 * LANES), BF16),
                        pltpu.VMEM((SCAN_GROUPS, ncg * RW_HEAD, 2 * LANES), F32),
                        pltpu.VMEM((SCAN_GROUPS, ncg * RW_HEAD // 2, 2 * LANES), BF16),
                        pltpu.VMEM((SCAN_GROUPS, ncg * RW_HEAD // 2, 2 * LANES), F32)],
        compiler_params=_params("parallel", "arbitrary"),
        name="scan_rwkv",
    )(*row_inputs, v, s0, ones)


SUBLANES = 8


def _sort_network(n):
    pairs, p = [], 1
    while p < n:
        k = p
        while k >= 1:
            for j in range(k % p, n - k, 2 * k):
                for i in range(min(k, n - j - k)):
                    if (i + j) // (2 * p) == (i + j + k) // (2 * p):
                        pairs.append((i + j, i + j + k))
            k //= 2
        p *= 2
    return pairs


def _exchange(v, i, j):
    v[i], v[j] = jnp.maximum(v[i], v[j]), jnp.minimum(v[i], v[j])


def _bitonic_merge(v):
    k = len(v) // 2
    while k >= 1:
        for i in range(len(v)):
            if not i & k:
                _exchange(v, i, i + k)
        k //= 2


def _merge_sublanes(v):
    for shift in (4, 2, 1):
        w = [pltpu.roll(x, shift, 0) for x in v]
        if len(v) < PK_TOPK:
            v = v + w[::-1]
        else:
            v = [jnp.maximum(v[i], w[PK_TOPK - 1 - i]) for i in range(PK_TOPK)]
        _bitonic_merge(v)
    return v


def _top16_of_rows(s):
    v = [s[SUBLANES * i:SUBLANES * (i + 1), :] for i in range(s.shape[0] // SUBLANES)]
    for i, j in _sort_network(len(v)):
        _exchange(v, i, j)
    return _merge_sublanes(v)


_CAND_PAIRS = [(a, b) for a in range(PK_TOPK) for b in range(PK_TOPK) if (a + 1) * (b + 1) <= PK_TOPK]


def _top16_pair_sums(t1, t2):
    sub = lax.broadcasted_iota(jnp.int32, t1[0].shape, 0)
    packed = []
    for k in range(0, len(_CAND_PAIRS), SUBLANES):
        tile = jnp.full(t1[0].shape, -jnp.inf, F32)
        for r, (a, b) in enumerate(_CAND_PAIRS[k:k + SUBLANES]):
            tile = jnp.where(sub == r, t1[a] + t2[b], tile)
        packed.append(tile)
    while len(packed) & (len(packed) - 1):
        packed.append(jnp.full(t1[0].shape, -jnp.inf, F32))
    for i, j in _sort_network(len(packed)):
        _exchange(packed, i, j)
    return _merge_sublanes(packed)


def _pquery_kernel(xn_ref, wq_ref, keys_ref, st_ref, thr_ref):
    q = _dot(xn_ref[...], wq_ref[...])
    for c in range(2 * PK_HEADS):
        qc = q[:, c * LANES:(c + 1) * LANES].astype(BF16)
        st_ref[c * LANES:(c + 1) * LANES, :] = lax.dot_general(
            keys_ref[c], qc, (((1,), (1,)), ((), ())), preferred_element_type=F32)

    for h in range(PK_HEADS):
        r1 = slice((2 * h) * LANES, (2 * h + 1) * LANES)
        r2 = slice((2 * h + 1) * LANES, (2 * h + 2) * LANES)
        s1, s2 = st_ref[r1, :], st_ref[r2, :]
        t1 = _top16_of_rows(s1)
        t2 = _top16_of_rows(s2)
        top = _top16_pair_sums(t1, t2)
        z = jnp.zeros_like(top[0])
        for m in top:
            z = z + jnp.exp(m - top[0])
        off = top[0] + jnp.log(z) + LN2
        st_ref[r1, :] = (s1 - off[0:1, :]) * LOG2E
        st_ref[r2, :] = s2 * LOG2E
        top = _top16_pair_sums([(t - off) * LOG2E for t in t1], [t * LOG2E for t in t2])
        thr_ref[h:h + 1, :] = top[-1][0:1, :]


def peer_query(xn, w_q, keys):
    n, d = xn.shape
    tm = _row_tile(n, 256)
    return pl.pallas_call(
        _pquery_kernel,
        grid=(n // tm,),
        in_specs=[pl.BlockSpec((tm, d), lambda i: (i, 0)), pl.BlockSpec((d, d), lambda i: (0, 0)),
                  pl.BlockSpec(keys.shape, lambda i: (0, 0, 0))],
        out_specs=[pl.BlockSpec((d, tm), lambda i: (0, i)), pl.BlockSpec((PK_HEADS, tm), lambda i: (0, i))],
        out_shape=[jax.ShapeDtypeStruct((d, n), F32), jax.ShapeDtypeStruct((PK_HEADS, n), F32)],
        compiler_params=_params("parallel"),
        name="peer_query",
    )(xn, w_q, keys)


PD_PARTS = 4


def _pdense_kernel(ec, xn_ref, u_ref, vt_ref, st_ref, thr_ref, y_ref, acc, xt_scr, hid_scr):
    j = pl.program_id(1)
    groups = ec // PK_NKEYS
    sub = 8

    @pl.when(j == 0)
    def _():
        acc[...] = jnp.zeros(acc.shape, F32)
        xt_scr[...] = xn_ref[...].astype(F32).T.astype(BF16)

    tm = xn_ref.shape[0]
    gpp = groups // PD_PARTS
    rows = gpp * PK_NKEYS
    thr = [jnp.broadcast_to(thr_ref[h:h + 1, :], (sub, tm)) for h in range(PK_HEADS)]

    def hidden(part):
        r = slice(part * rows, (part + 1) * rows)
        hid_scr[r, :] = _dot(u_ref[r, :], xt_scr[...])

    def finish(part):
        ps = []
        for ii in range(part * gpp, (part + 1) * gpp):
            gates = [None] * (PK_NKEYS // sub)
            for h in range(PK_HEADS):
                s1 = jnp.broadcast_to(st_ref[pl.ds(2 * h * LANES + j * groups + ii, 1), :], (sub, tm))
                for s in range(PK_NKEYS // sub):
                    c = s1 + st_ref[(2 * h + 1) * LANES + s * sub:(2 * h + 1) * LANES + (s + 1) * sub, :]
                    e = jnp.where(c >= thr[h], jnp.exp2(c), 0.0)
                    gates[s] = e if gates[s] is None else gates[s] + e
            hh = hid_scr[ii * PK_NKEYS:(ii + 1) * PK_NKEYS, :]
            ps.append((hh * (1.0 + lax.erf(hh * INV_SQRT2)) * jnp.concatenate(gates, axis=0)).astype(BF16))
        acc[...] += _dot(vt_ref[:, part * rows:(part + 1) * rows], jnp.concatenate(ps, axis=0))

    hidden(0)
    for part in range(PD_PARTS):
        if part + 1 < PD_PARTS:
            hidden(part + 1)
        finish(part)

    @pl.when(j == pl.num_programs(1) - 1)
    def _():
        y_ref[...] = acc[...].T


def peer_dense(xn, u, v_t, st, thr):
    n, d = xn.shape
    tm = _row_tile(n, 512)
    ec = 1024
    return pl.pallas_call(
        functools.partial(_pdense_kernel, ec),
        grid=(n // tm, PK_EXPERTS // ec),
        in_specs=[pl.BlockSpec((tm, d), lambda i, j: (i, 0)),
                  pl.BlockSpec((ec, d), lambda i, j: (j, 0)),
                  pl.BlockSpec((d, ec), lambda i, j: (0, j)),
                  pl.BlockSpec((d, tm), lambda i, j: (0, i)),
                  pl.BlockSpec((PK_HEADS, tm), lambda i, j: (0, i))],
        out_specs=pl.BlockSpec((tm, d), lambda i, j: (i, 0)),
        out_shape=jax.ShapeDtypeStruct((n, d), F32),
        scratch_shapes=[pltpu.VMEM((d, tm), F32), pltpu.VMEM((d, tm), BF16), pltpu.VMEM((ec, tm), F32)],
        compiler_params=_params("parallel", "arbitrary"),
        name="peer_dense",
    )(xn, u, v_t, st, thr)


def peer(xn, w_q, keys, u, v_t):
    st, thr = peer_query(xn, w_q, keys)
    return peer_dense(xn, u, v_t, st, thr)


def _pack_rwkv_state(s):
    b = s.shape[0]
    return s.reshape(b, N_CHUNK, 2, RW_HEAD, RW_HEAD).transpose(0, 1, 3, 2, 4).reshape(b, N_CHUNK, RW_HEAD, LANES)


def _unpack_rwkv_state(s):
    b = s.shape[0]
    return s.reshape(b, N_CHUNK, RW_HEAD, 2, RW_HEAD).transpose(0, 1, 3, 2, 4).reshape(b, 2 * N_CHUNK, RW_HEAD, RW_HEAD)


def _shifted(h, shift0):
    return jnp.concatenate([shift0[:, None].astype(h.dtype), h[:, :-1]], axis=1)


def kernel(x_prompt, x_sample, state_rwkv_wkv, state_rwkv_shift, state_hgrn, ln_mix_w, ln_ffn_w, ln_f_w, rw_mu, rw_w_rkv, rw_w0, rw_w1, rw_w2, rw_a0, rw_a1, rw_a2, rw_g1, rw_g2, rw_k_k, rw_k_a, rw_r_k, rw_ln_w, rw_ln_b, rw_w_o, hg_w_in, hg_lb, hg_norm_w, hg_w_o, pk_w_q, pk_keys, pk_u, pk_v):
    bp, tp, d = x_prompt.shape
    bs, ts, _ = x_sample.shape
    n_p, n_s = bp * tp, bs * ts
    ones64 = _block_ones(RW_HEAD)
    ones128 = _block_ones(HG_HEAD)
    ones64x4 = _block_ones(RW_HEAD, 2 * LANES)

    def split(a):
        return a[:n_p].reshape(bp, tp, d), a[n_p:].reshape(bs, ts, d)

    def both(fn, a_list, s0_p, s0_s):
        parts = [split(a) for a in a_list]
        yp, sp = fn([p[0] for p in parts], s0_p, True)
        ys, ss = fn([p[1] for p in parts], s0_s, False)
        return jnp.concatenate([yp.reshape(n_p, d), ys.reshape(n_s, d)], axis=0), sp, ss

    x = jnp.concatenate([x_prompt.reshape(n_p, d), x_sample.reshape(n_s, d)], axis=0)

    h = rmsnorm(x, ln_mix_w[0])
    h_p, h_s = split(h)
    xp = jnp.concatenate([_shifted(h_p, jnp.zeros((bp, d), F32)).reshape(n_p, d),
                          _shifted(h_s, state_rwkv_shift[0]).reshape(n_s, d)], axis=0)
    rkv = rwkv_rkv(h, xp, rw_mu[0, :3], rw_w_rkv[0].astype(BF16))
    lora_pad = LANES - rw_w1.shape[-1]
    pad_in = lambda w: jnp.pad(w, ((0, 0), (0, lora_pad))).astype(BF16)
    pad_out = lambda w: jnp.pad(w, ((0, lora_pad), (0, 0))).astype(BF16)
    dec, nkk, bb, k2, g, rkb = rwkv_prep(
        h, xp, rkv[0], rkv[1], rw_mu[0, 3:], pad_in(rw_w1[0]), pad_out(rw_w2[0]), pad_in(rw_a1[0]),
        pad_out(rw_a2[0]), rw_g1[0].astype(BF16), rw_g2[0].astype(BF16), rw_w0[0], rw_a0[0], rw_k_k[0],
        rw_k_a[0], rw_r_k[0].reshape(d), ones64)

    def rw_scan(rows_in, s0, prompt):
        return rwkv_recurrence(rows_in[:-1], rows_in[-1], s0, ones64x4, 4, min(RW_HEAD, rows_in[0].shape[1]))

    y, wkv_p, wkv_s = both(rw_scan, [dec, nkk, bb, k2, rkv[0], rkv[2]],
                           jnp.zeros((bp, N_CHUNK, RW_HEAD, LANES), F32), _pack_rwkv_state(state_rwkv_wkv[0]))
    x, xn = rwkv_post(y, rkv[2], rkb, g, x, rw_ln_w[0], rw_ln_b[0], rw_w_o[0].astype(BF16), ln_ffn_w[0], ones64)
    yp = peer(xn, pk_w_q[0].astype(BF16), pk_keys[0].reshape(2 * PK_HEADS, PK_NKEYS, LANES).astype(BF16),
              pk_u[0].astype(BF16), pk_v[0].astype(BF16).T)
    x, h1 = add_norm(x, yp, ln_mix_w[1], BF16)

    lb = jax.nn.softmax(hg_lb.astype(F32), axis=0)
    lb = (jnp.cumsum(lb, axis=0) - lb[0])[1]
    w_in = hg_w_in[0].astype(BF16)
    q = hgrn_in(h1, w_in[:, :d], lb, "silu")
    f = hgrn_in(h1, w_in[:, d:2 * d], lb, "forget")
    vin = hgrn_in(h1, w_in[:, 2 * d:3 * d], lb, "none")
    sgate = hgrn_in(h1, w_in[:, 3 * d:], lb, "silu")

    o_p, hg_p = hgrn_chunked(f, q, vin, jnp.zeros((bp, N_CHUNK, HG_HEAD, LANES), F32), 0, bp, tp, 16, LANES,
                             ones128)
    o_s, hg_s = hgrn_chunked(f, q, vin, jnp.swapaxes(state_hgrn[0], -1, -2), n_p, bs, ts, ts, 8 * ts, ones128)
    o = jnp.concatenate([o_p, o_s], axis=0)
    x, xn = hgrn_post(o, sgate, x, jnp.tile(hg_norm_w[0], N_CHUNK), hg_w_o[0].astype(BF16), ln_ffn_w[1], ones128)
    yp = peer(xn, pk_w_q[1].astype(BF16), pk_keys[1].reshape(2 * PK_HEADS, PK_NKEYS, LANES).astype(BF16),
              pk_u[1].astype(BF16), pk_v[1].astype(BF16).T)
    _, out = add_norm(x, yp, ln_f_w, F32)

    y_p, y_s = split(out)
    return (y_p, y_s,
            _unpack_rwkv_state(wkv_p)[None], h_p[:, -1][None], jnp.swapaxes(hg_p, -1, -2)[None],
            _unpack_rwkv_state(wkv_s)[None], h_s[:, -1][None], jnp.swapaxes(hg_s, -1, -2)[None])
```

```python
import functools
import math

import jax
import jax.numpy as jnp
from jax import lax
from jax.experimental import pallas as pl
from jax.experimental.pallas import tpu as pltpu

F32 = jnp.float32
BF16 = jnp.bfloat16

D_MODEL = 2048
LANES = 128
N_CHUNK = D_MODEL // LANES
RW_HEAD = 64
HG_HEAD = 128
RW_GN_EPS = 64e-5
RMS_EPS = 1e-6
PK_HEADS = 8
PK_NKEYS = 128
PK_TOPK = 16
PK_EXPERTS = PK_NKEYS * PK_NKEYS
INV_SQRT2 = 1.0 / math.sqrt(2.0)
LN2 = math.log(2.0)
LOG2E = 1.0 / LN2
VMEM_LIMIT = 56 * 1024 * 1024


def _params(*sem, flags=None):
    return pltpu.CompilerParams(dimension_semantics=sem, vmem_limit_bytes=VMEM_LIMIT, flags=flags)


def _row_tile(n, cap):
    t = cap
    while n % t:
        t //= 2
    return t


def _dot(a, b):
    return jnp.dot(a, b, preferred_element_type=F32)


def _rms(x, w):
    return x * lax.rsqrt(jnp.mean(x * x, axis=-1, keepdims=True) + RMS_EPS) * w


def _group_sum(x, ones_ref):
    tm, d = x.shape
    nch = d // LANES
    xs = jnp.concatenate([x[:, j * LANES:(j + 1) * LANES] for j in range(nch)], axis=0)
    hi = xs.astype(BF16)
    lo = (xs - hi.astype(F32)).astype(BF16)
    w = ones_ref[...]
    r = _dot(hi, w) + _dot(lo, w)
    return jnp.concatenate([r[j * tm:(j + 1) * tm] for j in range(nch)], axis=1)


def _block_ones(group, size=LANES):
    i = jnp.arange(size) // group
    return (i[:, None] == i[None, :]).astype(BF16)


def _rms_kernel(x_ref, w_ref, o_ref):
    o_ref[...] = _rms(x_ref[...], w_ref[...]).astype(o_ref.dtype)


def rmsnorm(x, w, out_dtype=F32):
    n, d = x.shape
    tm = _row_tile(n, 512)
    return pl.pallas_call(
        _rms_kernel,
        grid=(n // tm,),
        in_specs=[pl.BlockSpec((tm, d), lambda i: (i, 0)), pl.BlockSpec((1, d), lambda i: (0, 0))],
        out_specs=pl.BlockSpec((tm, d), lambda i: (i, 0)),
        out_shape=jax.ShapeDtypeStruct((n, d), out_dtype),
        compiler_params=_params("parallel"),
        name="rmsnorm",
    )(x, w.reshape(1, d))


def _addnorm_kernel(x_ref, y_ref, w_ref, s_ref, n_ref):
    s = x_ref[...] + y_ref[...]
    s_ref[...] = s
    n_ref[...] = _rms(s, w_ref[...]).astype(n_ref.dtype)


def add_norm(x, y, w, norm_dtype):
    n, d = x.shape
    tm = _row_tile(n, 512)
    row = pl.BlockSpec((tm, d), lambda i: (i, 0))
    return pl.pallas_call(
        _addnorm_kernel,
        grid=(n // tm,),
        in_specs=[row, row, pl.BlockSpec((1, d), lambda i: (0, 0))],
        out_specs=[row, row],
        out_shape=[jax.ShapeDtypeStruct((n, d), F32), jax.ShapeDtypeStruct((n, d), norm_dtype)],
        compiler_params=_params("parallel"),
        name="add_norm",
    )(x, y, w.reshape(1, d))


def _rkv_kernel(h_ref, xp_ref, mu_ref, w_ref, o_ref):
    h = h_ref[...]
    xm = h + (xp_ref[...] - h) * mu_ref[0]
    o_ref[0] = _dot(xm.astype(BF16), w_ref[0])


def rwkv_rkv(h, xp, mu3, w3):
    n, d = h.shape
    tm = _row_tile(n, 512)
    row = pl.BlockSpec((tm, d), lambda s, i: (i, 0))
    return pl.pallas_call(
        _rkv_kernel,
        grid=(3, n // tm),
        in_specs=[row, row,
                  pl.BlockSpec((1, 1, d), lambda s, i: (s, 0, 0)),
                  pl.BlockSpec((1, d, d), lambda s, i: (s, 0, 0))],
        out_specs=pl.BlockSpec((1, tm, d), lambda s, i: (s, i, 0)),
        out_shape=jax.ShapeDtypeStruct((3, n, d), F32),
        compiler_params=_params("arbitrary", "arbitrary"),
        name="rwkv_rkv",
    )(h, xp, mu3.reshape(3, 1, d), w3)


def _softplus(z):
    return jnp.maximum(z, 0.0) + jnp.log1p(jnp.exp(-jnp.abs(z)))


def _rwprep_kernel(h_ref, xp_ref, r_ref, k_ref, mu_ref, w1_ref, w2_ref, a1_ref, a2_ref, g1_ref, g2_ref,
                   w0_ref, a0_ref, kk_ref, ka_ref, rk_ref, ones_ref,
                   dec_o, nkk_o, b_o, k2_o, g_o, rkb_o):
    h = h_ref[...]
    dx = xp_ref[...] - h
    xw = (h + dx * mu_ref[0]).astype(BF16)
    xa = (h + dx * mu_ref[1]).astype(BF16)
    xg = (h + dx * mu_ref[2]).astype(BF16)
    wl = w0_ref[...] + _dot(jnp.tanh(_dot(xw, w1_ref[...])).astype(BF16), w2_ref[...])
    wv = -_softplus(-wl) - 0.5
    dec_o[...] = jnp.exp(-jnp.exp(wv))
    a = jax.nn.sigmoid(a0_ref[...] + _dot(_dot(xa, a1_ref[...]).astype(BF16), a2_ref[...]))
    g_o[...] = _dot(jax.nn.sigmoid(_dot(xg, g1_ref[...])).astype(BF16), g2_ref[...])
    k = k_ref[...]
    kk = k * kk_ref[...]
    kk = kk * lax.rsqrt(jnp.maximum(_group_sum(kk * kk, ones_ref), 1e-24))
    k2 = k * (1.0 + (a - 1.0) * ka_ref[...])
    nkk_o[...] = -kk
    b_o[...] = kk * a
    k2_o[...] = k2
    rkb_o[...] = _group_sum(r_ref[...] * k2 * rk_ref[...], ones_ref)


def rwkv_prep(h, xp, r, k, mu3, w1, w2, a1, a2, g1, g2, w0, a0, k_k, k_a, r_k, ones64):
    n, d = h.shape
    tm = _row_tile(n, 128)
    row = pl.BlockSpec((tm, d), lambda i: (i, 0))
    vec = pl.BlockSpec((1, d), lambda i: (0, 0))

    def full(a):
        return pl.BlockSpec(a.shape, lambda i: (0,) * a.ndim)

    mu3 = mu3.reshape(3, 1, d)
    vecs = [v.reshape(1, d) for v in (w0, a0, k_k, k_a, r_k)]
    return pl.pallas_call(
        _rwprep_kernel,
        grid=(n // tm,),
        in_specs=[row, row, row, row, full(mu3), full(w1), full(w2), full(a1), full(a2), full(g1), full(g2),
                  vec, vec, vec, vec, vec, full(ones64)],
        out_specs=[row] * 6,
        out_shape=[jax.ShapeDtypeStruct((n, d), F32)] * 6,
        compiler_params=_params("parallel"),
        name="rwkv_prep",
    )(h, xp, r, k, mu3, w1, w2, a1, a2, g1, g2, *vecs, ones64)


def _rwpost_kernel(y_ref, v_ref, rkb_ref, g_ref, x_ref, lnw_ref, lnb_ref, wo_ref, nw_ref, ones_ref,
                   x1_o, xn_o):
    y = y_ref[...]
    d = y - _group_sum(y, ones_ref) * (1.0 / RW_HEAD)
    var = _group_sum(d * d, ones_ref) * (1.0 / RW_HEAD)
    z = d * lax.rsqrt(var + RW_GN_EPS) * lnw_ref[...] + lnb_ref[...] + rkb_ref[...] * v_ref[...]
    x1 = x_ref[...] + _dot((z * g_ref[...]).astype(BF16), wo_ref[...])
    x1_o[...] = x1
    xn_o[...] = _rms(x1, nw_ref[...]).astype(xn_o.dtype)


def rwkv_post(y, v, rkb, g, x, ln_w, ln_b, w_o, norm_w, ones64):
    n, d = y.shape
    tm = _row_tile(n, 128)
    row = pl.BlockSpec((tm, d), lambda i: (i, 0))
    vec = pl.BlockSpec((1, d), lambda i: (0, 0))
    return pl.pallas_call(
        _rwpost_kernel,
        grid=(n // tm,),
        in_specs=[row, row, row, row, row, vec, vec, pl.BlockSpec((d, d), lambda i: (0, 0)), vec,
                  pl.BlockSpec((LANES, LANES), lambda i: (0, 0))],
        out_specs=[row, row],
        out_shape=[jax.ShapeDtypeStruct((n, d), F32), jax.ShapeDtypeStruct((n, d), BF16)],
        compiler_params=_params("parallel"),
        name="rwkv_post",
    )(y, v, rkb, g, x, ln_w.reshape(1, d), ln_b.reshape(1, d), w_o, norm_w.reshape(1, d), ones64)


def _hgin_kernel(kind, a_ref, w_ref, lb_ref, o_ref):
    o = _dot(a_ref[...], w_ref[...])
    if kind == "silu":
        o = o * jax.nn.sigmoid(o)
    elif kind == "forget":
        lb = lb_ref[...]
        o = lb + (1.0 - lb) * jax.nn.sigmoid(o)
    o_ref[...] = o


def hgrn_in(a, w, lb, kind):
    n, d = a.shape
    tm = _row_tile(n, 512)
    return pl.pallas_call(
        functools.partial(_hgin_kernel, kind),
        grid=(n // tm,),
        in_specs=[pl.BlockSpec((tm, d), lambda i: (i, 0)), pl.BlockSpec((d, d), lambda i: (0, 0)),
                  pl.BlockSpec((1, d), lambda i: (0, 0))],
        out_specs=pl.BlockSpec((tm, d), lambda i: (i, 0)),
        out_shape=jax.ShapeDtypeStruct((n, d), F32),
        compiler_params=_params("parallel"),
        name="hgrn_in_" + kind,
    )(a, w, lb.reshape(1, d))


def _hgpost_kernel(o_ref, sg_ref, x_ref, nw_ref, wo_ref, fw_ref, ones_ref, x3_o, xn_o):
    o = o_ref[...]
    ms = _group_sum(o * o, ones_ref) * (1.0 / HG_HEAD)
    z = o * lax.rsqrt(ms + RMS_EPS) * nw_ref[...] * sg_ref[...]
    x3 = x_ref[...] + _dot(z.astype(BF16), wo_ref[...])
    x3_o[...] = x3
    xn_o[...] = _rms(x3, fw_ref[...]).astype(xn_o.dtype)


def hgrn_post(o, sgate, x, norm_w_tiled, w_o, ffn_w, ones128):
    n, d = o.shape
    tm = _row_tile(n, 256)
    row = pl.BlockSpec((tm, d), lambda i: (i, 0))
    vec = pl.BlockSpec((1, d), lambda i: (0, 0))
    return pl.pallas_call(
        _hgpost_kernel,
        grid=(n // tm,),
        in_specs=[row, row, row, vec, pl.BlockSpec((d, d), lambda i: (0, 0)), vec,
                  pl.BlockSpec((LANES, LANES), lambda i: (0, 0))],
        out_specs=[row, row],
        out_shape=[jax.ShapeDtypeStruct((n, d), F32), jax.ShapeDtypeStruct((n, d), BF16)],
        compiler_params=_params("parallel"),
        name="hgrn_post",
    )(o, sgate, x, norm_w_tiled.reshape(1, d), w_o, ffn_w.reshape(1, d), ones128)


_NEG_BIG = -1e30


def _hgchunk_kernel(c, shared, hpg, f_ref, q_ref, v_ref, s0_ref, tril_ref, ones_ref, o_ref, st_ref, s_scr):
    tr = f_ref.shape[0]
    nseg = tr // c
    cp = max(c, 16)
    tile = pl.program_id(2) if shared else 0

    if shared:
        @pl.when(tile == 0)
        def _():
            s_scr[...] = s0_ref[...]
    else:
        s_scr[...] = s0_ref[...]

    row_in_chunk = lax.broadcasted_iota(jnp.int32, (tr, LANES), 0) & (c - 1)
    tril = tril_ref[...]

    def rows16(x):
        if cp == c:
            return x.astype(BF16)
        return jnp.concatenate([x, jnp.zeros((cp - c, x.shape[1]), F32)], axis=0).astype(BF16)

    for hh in range(hpg):
        sl = slice(hh * LANES, (hh + 1) * LANES)
        f, q, v = f_ref[:, sl], q_ref[:, sl], v_ref[:, sl]
        g = jnp.log(f)
        g_hi = g.astype(BF16)
        g_mid = (g - g_hi.astype(F32)).astype(BF16)
        g_lo = (g - g_hi.astype(F32) - g_mid.astype(F32)).astype(BF16)
        b = _dot(tril, g_hi) + _dot(tril, g_mid) + _dot(tril, g_lo)
        k = 1.0 - f
        terms = [(q * k).astype(BF16)]
        for d in range(1, c):
            arg = jnp.where(row_in_chunk >= d, b - pltpu.roll(b, d, 0), _NEG_BIG)
            terms.append((jnp.exp(arg) * q * pltpu.roll(k, d, 0)).astype(BF16))
        coef = _dot(jnp.concatenate(terms, axis=0), ones_ref[...])
        o = coef[0:tr] * v
        for d in range(1, c):
            o = o + coef[d * tr:(d + 1) * tr] * pltpu.roll(v, d, 0)
        qb = q * jnp.exp(b)
        decays, updates = [], []
        for seg in range(nseg):
            r = slice(seg * c, (seg + 1) * c)
            b_end = b[seg * c + c - 1:seg * c + c, :]
            kd = rows16(k[r] * jnp.exp(b_end - b[r]))
            decays.append(jnp.exp(b_end))
            updates.append(lax.dot_general(rows16(v[r]), kd, (((0,), (0,)), ((), ())),
                                           preferred_element_type=F32))
        starts = []
        for seg in range(nseg):
            slot = 0 if shared else seg
            s_old = s_scr[slot, hh]
            starts.append(s_old.astype(BF16))
            s_scr[slot, hh] = s_old * decays[seg] + updates[seg]
        inter = []
        for seg in range(nseg):
            oi = lax.dot_general(rows16(qb[seg * c:(seg + 1) * c]), starts[seg], (((1,), (1,)), ((), ())),
                                 preferred_element_type=F32)
            inter.append(oi[:c])
        o_ref[:, sl] = o + jnp.concatenate(inter, axis=0)

    if shared:
        @pl.when(tile == pl.num_programs(2) - 1)
        def _():
            st_ref[...] = s_scr[...]
    else:
        st_ref[...] = s_scr[...]


def hgrn_chunked(f, q, v, s0, row0, n_seq, t_len, c, tr, ones128):
    d = f.shape[1]
    hpg = 4
    ngrp = N_CHUNK // hpg
    shared = t_len >= tr
    i = jnp.arange(tr)
    tril = ((i[:, None] // c == i[None, :] // c) & (i[None, :] <= i[:, None])).astype(BF16)
    r0 = row0 // tr
    if shared:
        tiles = t_len // tr
        grid = (n_seq, ngrp, tiles)
        seq = pl.BlockSpec((tr, hpg * LANES), lambda s, g, t: (r0 + s * tiles + t, g))
        oseq = pl.BlockSpec((tr, hpg * LANES), lambda s, g, t: (s * tiles + t, g))
        state = pl.BlockSpec((1, hpg, HG_HEAD, LANES), lambda s, g, t: (s, g, 0, 0))
        const = lambda s, g, t: (0, 0)
        sem = ("parallel", "parallel", "arbitrary")
        nslot = 1
    else:
        nslot = tr // t_len
        grid = (n_seq // nslot, ngrp)
        seq = pl.BlockSpec((tr, hpg * LANES), lambda s, g: (r0 + s, g))
        oseq = pl.BlockSpec((tr, hpg * LANES), lambda s, g: (s, g))
        state = pl.BlockSpec((nslot, hpg, HG_HEAD, LANES), lambda s, g: (s, g, 0, 0))
        const = lambda s, g: (0, 0)
        sem = ("parallel", "parallel")
    return pl.pallas_call(
        functools.partial(_hgchunk_kernel, c, shared, hpg),
        grid=grid,
        in_specs=[seq, seq, seq, state, pl.BlockSpec((tr, tr), const), pl.BlockSpec((LANES, LANES), const)],
        out_specs=[oseq, state],
        out_shape=[jax.ShapeDtypeStruct((n_seq * t_len, d), F32), jax.ShapeDtypeStruct(s0.shape, F32)],
        scratch_shapes=[pltpu.VMEM((nslot, hpg, HG_HEAD, LANES), F32)],
        compiler_params=_params(*sem),
        name="hgrn_chunked",
    )(f, q, v, s0, tril, ones128)


SCAN_GROUPS = 2


def _scan_kernel(bg, tb, *refs):
    w_ref, nkk_ref, b_ref, k_ref, r_ref, v_ref = (refs[i * bg:(i + 1) * bg] for i in range(6))
    s0_ref, ones_ref, y_ref, st_ref, s_scr, vt_scr, yt_scr, l1, r1, l2, r2 = refs[6 * bg:]
    rows = RW_HEAD
    nc = bg * N_CHUNK
    ncg = nc // SCAN_GROUPS
    tblk = pl.program_id(1)
    lane = lax.broadcasted_iota(jnp.int32, (rows, LANES), 1)
    low = lane < RW_HEAD

    @pl.when(tblk == 0)
    def _():
        for c in range(nc):
            s_scr[c] = s0_ref[c // N_CHUNK, c % N_CHUNK]

    def lanes_of(c):
        return slice((c % N_CHUNK) * LANES, (c % N_CHUNK + 1) * LANES)

    def row(ref, c, t):
        return ref[c // N_CHUNK][t, c % N_CHUNK:c % N_CHUNK + 1, :]

    def pair_slot(cl):
        return slice((cl // 2) * rows, (cl // 2 + 1) * rows), slice((cl % 2) * LANES, (cl % 2 + 1) * LANES)

    for c in range(nc):
        vb = v_ref[c // N_CHUNK][:, lanes_of(c)]
        vb = jnp.concatenate([vb, jnp.zeros((LANES - tb, LANES), F32)], axis=0)
        vt = vb.T
        vt_scr[c] = jnp.where(low, vt[:RW_HEAD], pltpu.roll(vt[RW_HEAD:], RW_HEAD, 1))
        yt_scr[c] = jnp.zeros((rows, LANES), F32)

    def reduce_state(g, t, onehot):
        for cl in range(ncg):
            c = g * ncg + cl
            p = (s_scr[c] * row(nkk_ref, c, t)).astype(BF16)
            vsel = jnp.where(onehot, vt_scr[c], 0.0).astype(BF16)
            l1[g, cl * rows:(cl + 1) * rows, :] = jnp.concatenate([p, vsel], axis=1)
        r1[g] = _dot(l1[g], ones_ref[...])

    def update(g, t):
        for cl in range(ncg):
            c = g * ncg + cl
            rs, ls = pair_slot(cl)
            sa = r1[g, cl * rows:(cl + 1) * rows, 0:LANES]
            vc = r1[g, cl * rows:(cl + 1) * rows, LANES:2 * LANES]
            s_new = s_scr[c] * row(w_ref, c, t) + sa * row(b_ref, c, t) + vc * row(k_ref, c, t)
            s_scr[c] = s_new
            l2[g, rs, ls] = (s_new * row(r_ref, c, t)).astype(BF16)
        r2[g] = _dot(l2[g], ones_ref[...])

    def collect(g, onehot):
        for cl in range(ncg):
            rs, ls = pair_slot(cl)
            c = g * ncg + cl
            yt_scr[c] = jnp.where(onehot, r2[g, rs, ls], yt_scr[c])

    def step(t, carry):
        onehot = (lane & (RW_HEAD - 1)) == t
        for g in range(SCAN_GROUPS):
            reduce_state(g, t, onehot)
        for g in range(SCAN_GROUPS):
            update(g, t)
        for g in range(SCAN_GROUPS):
            collect(g, onehot)
        return carry

    lax.fori_loop(0, tb, step, 0, unroll=2)

    for c in range(nc):
        y0 = yt_scr[c]
        yt = jnp.concatenate([jnp.where(low, y0, 0.0), jnp.where(low, pltpu.roll(y0, RW_HEAD, 1), 0.0)], axis=0)
        y_ref[c // N_CHUNK, :, lanes_of(c)] = yt.T[:tb]

    @pl.when(tblk == pl.num_programs(1) - 1)
    def _():
        for c in range(nc):
            st_ref[c // N_CHUNK, c % N_CHUNK] = s_scr[c]


def rwkv_recurrence(row_inputs, v, row0, b, t, s0, ones, bg, tb):
    d = v.shape[1]
    assert tb <= RW_HEAD and t % tb == 0 and b % bg == 0 and row0 % tb == 0
    nc = bg * N_CHUNK
    ncg = nc // SCAN_GROUPS
    tiles = t // tb

    def member(block, m):
        return pl.BlockSpec(block, lambda i, j: (row0 // tb + (i * bg + m) * tiles + j,) + (0,) * (len(block) - 1))

    seq = pl.BlockSpec((bg, tb, d), lambda i, j: (i, j, 0))
    state = pl.BlockSpec((bg, N_CHUNK, RW_HEAD, LANES), lambda i, j: (i, 0, 0, 0))
    operands, specs = [], []
    for a in row_inputs:
        operands += [a] * bg
        specs += [member((tb, N_CHUNK, LANES), m) for m in range(bg)]
    operands += [v] * bg
    specs += [member((tb, d), m) for m in range(bg)]
    return pl.pallas_call(
        functools.partial(_scan_kernel, bg, tb),
        grid=(b // bg, tiles),
        in_specs=specs + [state, pl.BlockSpec((2 * LANES, 2 * LANES), lambda i, j: (0, 0))],
        out_specs=[seq, state],
        out_shape=[jax.ShapeDtypeStruct((b, t, d), F32), jax.ShapeDtypeStruct(s0.shape, F32)],
        scratch_shapes=[pltpu.VMEM((nc, RW_HEAD, LANES), F32),
                        pltpu.VMEM((nc, RW_HEAD, LANES), F32),
                        pltpu.VMEM((nc, RW_HEAD, LANES), F32),
                        pltpu.VMEM((SCAN_GROUPS, ncg * RW_HEAD, 2 * LANES), BF16),
                        pltpu.VMEM((SCAN_GROUPS, ncg * RW_HEAD, 2 * LANES), F32),
                        pltpu.VMEM((SCAN_GROUPS, ncg * RW_HEAD // 2, 2 * LANES), BF16),
                        pltpu.VMEM((SCAN_GROUPS, ncg * RW_HEAD // 2, 2 * LANES), F32)],
        compiler_params=_params("parallel", "arbitrary"),
        name="scan_rwkv",
    )(*operands, s0, ones)


SUBLANES = 8


def _sort_network(n):
    pairs, p = [], 1
    while p < n:
        k = p
        while k >= 1:
            for j in range(k % p, n - k, 2 * k):
                for i in range(min(k, n - j - k)):
                    if (i + j) // (2 * p) == (i + j + k) // (2 * p):
                        pairs.append((i + j, i + j + k))
            k //= 2
        p *= 2
    return pairs


def _exchange(v, i, j):
    v[i], v[j] = jnp.maximum(v[i], v[j]), jnp.minimum(v[i], v[j])


def _bitonic_merge(v):
    k = len(v) // 2
    while k >= 1:
        for i in range(len(v)):
            if not i & k:
                _exchange(v, i, i + k)
        k //= 2


def _merge_sublanes(v):
    for shift in (4, 2, 1):
        w = [pltpu.roll(x, shift, 0) for x in v]
        if len(v) < PK_TOPK:
            v = v + w[::-1]
        else:
            v = [jnp.maximum(v[i], w[PK_TOPK - 1 - i]) for i in range(PK_TOPK)]
        _bitonic_merge(v)
    return v


def _top16_of_rows(s):
    v = [s[SUBLANES * i:SUBLANES * (i + 1), :] for i in range(s.shape[0] // SUBLANES)]
    for i, j in _sort_network(len(v)):
        _exchange(v, i, j)
    return _merge_sublanes(v)


_CAND_PAIRS = [(a, b) for a in range(PK_TOPK) for b in range(PK_TOPK) if (a + 1) * (b + 1) <= PK_TOPK]


def _top16_pair_sums(t1, t2):
    sub = lax.broadcasted_iota(jnp.int32, t1[0].shape, 0)
    packed = []
    for k in range(0, len(_CAND_PAIRS), SUBLANES):
        tile = jnp.full(t1[0].shape, -jnp.inf, F32)
        for r, (a, b) in enumerate(_CAND_PAIRS[k:k + SUBLANES]):
            tile = jnp.where(sub == r, t1[a] + t2[b], tile)
        packed.append(tile)
    while len(packed) & (len(packed) - 1):
        packed.append(jnp.full(t1[0].shape, -jnp.inf, F32))
    for i, j in _sort_network(len(packed)):
        _exchange(packed, i, j)
    return _merge_sublanes(packed)


def _pquery_kernel(xn_ref, wq_ref, keys_ref, st_ref, thr_ref):
    q = _dot(xn_ref[...], wq_ref[...])
    for c in range(2 * PK_HEADS):
        qc = q[:, c * LANES:(c + 1) * LANES].astype(BF16)
        st_ref[c * LANES:(c + 1) * LANES, :] = lax.dot_general(
            keys_ref[c], qc, (((1,), (1,)), ((), ())), preferred_element_type=F32)

    for h in range(PK_HEADS):
        r1 = slice((2 * h) * LANES, (2 * h + 1) * LANES)
        r2 = slice((2 * h + 1) * LANES, (2 * h + 2) * LANES)
        s1, s2 = st_ref[r1, :], st_ref[r2, :]
        t1 = _top16_of_rows(s1)
        t2 = _top16_of_rows(s2)
        top = _top16_pair_sums(t1, t2)
        z = jnp.zeros_like(top[0])
        for m in top:
            z = z + jnp.exp(m - top[0])
        off = top[0] + jnp.log(z) + LN2
        st_ref[r1, :] = (s1 - off[0:1, :]) * LOG2E
        st_ref[r2, :] = s2 * LOG2E
        top = _top16_pair_sums([(t - off) * LOG2E for t in t1], [t * LOG2E for t in t2])
        thr_ref[h:h + 1, :] = top[-1][0:1, :]


def peer_query(xn, w_q, keys):
    n, d = xn.shape
    tm = _row_tile(n, 256)
    return pl.pallas_call(
        _pquery_kernel,
        grid=(n // tm,),
        in_specs=[pl.BlockSpec((tm, d), lambda i: (i, 0)), pl.BlockSpec((d, d), lambda i: (0, 0)),
                  pl.BlockSpec(keys.shape, lambda i: (0, 0, 0))],
        out_specs=[pl.BlockSpec((d, tm), lambda i: (0, i)), pl.BlockSpec((PK_HEADS, tm), lambda i: (0, i))],
        out_shape=[jax.ShapeDtypeStruct((d, n), F32), jax.ShapeDtypeStruct((PK_HEADS, n), F32)],
        compiler_params=_params("parallel"),
        name="peer_query",
    )(xn, w_q, keys)


PD_PARTS = 4


def _pdense_kernel(ec, xn_ref, u_ref, vt_ref, st_ref, thr_ref, y_ref, acc, xt_scr, hid_scr):
    j = pl.program_id(1)
    groups = ec // PK_NKEYS
    sub = 8

    @pl.when(j == 0)
    def _():
        acc[...] = jnp.zeros(acc.shape, F32)
        xt_scr[...] = xn_ref[...].astype(F32).T.astype(BF16)

    tm = xn_ref.shape[0]
    gpp = groups // PD_PARTS
    rows = gpp * PK_NKEYS
    thr = [jnp.broadcast_to(thr_ref[h:h + 1, :], (sub, tm)) for h in range(PK_HEADS)]

    def hidden(part):
        r = slice(part * rows, (part + 1) * rows)
        hid_scr[r, :] = _dot(u_ref[r, :], xt_scr[...])

    def finish(part):
        ps = []
        for ii in range(part * gpp, (part + 1) * gpp):
            gates = [None] * (PK_NKEYS // sub)
            for h in range(PK_HEADS):
                s1 = jnp.broadcast_to(st_ref[pl.ds(2 * h * LANES + j * groups + ii, 1), :], (sub, tm))
                for s in range(PK_NKEYS // sub):
                    c = s1 + st_ref[(2 * h + 1) * LANES + s * sub:(2 * h + 1) * LANES + (s + 1) * sub, :]
                    e = jnp.where(c >= thr[h], jnp.exp2(c), 0.0)
                    gates[s] = e if gates[s] is None else gates[s] + e
            hh = hid_scr[ii * PK_NKEYS:(ii + 1) * PK_NKEYS, :]
            ps.append((hh * (1.0 + lax.erf(hh * INV_SQRT2)) * jnp.concatenate(gates, axis=0)).astype(BF16))
        acc[...] += _dot(vt_ref[:, part * rows:(part + 1) * rows], jnp.concatenate(ps, axis=0))

    hidden(0)
    for part in range(PD_PARTS):
        if part + 1 < PD_PARTS:
            hidden(part + 1)
        finish(part)

    @pl.when(j == pl.num_programs(1) - 1)
    def _():
        y_ref[...] = acc[...].T


def peer_dense(xn, u, v_t, st, thr):
    n, d = xn.shape
    tm = _row_tile(n, 512)
    ec = 1024
    return pl.pallas_call(
        functools.partial(_pdense_kernel, ec),
        grid=(n // tm, PK_EXPERTS // ec),
        in_specs=[pl.BlockSpec((tm, d), lambda i, j: (i, 0)),
                  pl.BlockSpec((ec, d), lambda i, j: (j, 0)),
                  pl.BlockSpec((d, ec), lambda i, j: (0, j)),
                  pl.BlockSpec((d, tm), lambda i, j: (0, i)),
                  pl.BlockSpec((PK_HEADS, tm), lambda i, j: (0, i))],
        out_specs=pl.BlockSpec((tm, d), lambda i, j: (i, 0)),
        out_shape=jax.ShapeDtypeStruct((n, d), F32),
        scratch_shapes=[pltpu.VMEM((d, tm), F32), pltpu.VMEM((d, tm), BF16), pltpu.VMEM((ec, tm), F32)],
        compiler_params=_params("parallel", "arbitrary"),
        name="peer_dense",
    )(xn, u, v_t, st, thr)


def peer(xn, w_q, keys, u, v_t):
    st, thr = peer_query(xn, w_q, keys)
    return peer_dense(xn, u, v_t, st, thr)


def _pack_rwkv_state(s):
    b = s.shape[0]
    return s.reshape(b, N_CHUNK, 2, RW_HEAD, RW_HEAD).transpose(0, 1, 3, 2, 4).reshape(b, N_CHUNK, RW_HEAD, LANES)


def _unpack_rwkv_state(s):
    b = s.shape[0]
    return s.reshape(b, N_CHUNK, RW_HEAD, 2, RW_HEAD).transpose(0, 1, 3, 2, 4).reshape(b, 2 * N_CHUNK, RW_HEAD, RW_HEAD)


def _shifted(h, shift0):
    return jnp.concatenate([shift0[:, None].astype(h.dtype), h[:, :-1]], axis=1)


def kernel(x_prompt, x_sample, state_rwkv_wkv, state_rwkv_shift, state_hgrn, ln_mix_w, ln_ffn_w, ln_f_w, rw_mu, rw_w_rkv, rw_w0, rw_w1, rw_w2, rw_a0, rw_a1, rw_a2, rw_g1, rw_g2, rw_k_k, rw_k_a, rw_r_k, rw_ln_w, rw_ln_b, rw_w_o, hg_w_in, hg_lb, hg_norm_w, hg_w_o, pk_w_q, pk_keys, pk_u, pk_v):
    bp, tp, d = x_prompt.shape
    bs, ts, _ = x_sample.shape
    n_p, n_s = bp * tp, bs * ts
    ones64 = _block_ones(RW_HEAD)
    ones128 = _block_ones(HG_HEAD)
    ones64x4 = _block_ones(RW_HEAD, 2 * LANES)

    def split(a):
        return a[:n_p].reshape(bp, tp, d), a[n_p:].reshape(bs, ts, d)

    x = jnp.concatenate([x_prompt.reshape(n_p, d), x_sample.reshape(n_s, d)], axis=0)

    h = rmsnorm(x, ln_mix_w[0])
    h_p, h_s = split(h)
    xp = jnp.concatenate([_shifted(h_p, jnp.zeros((bp, d), F32)).reshape(n_p, d),
                          _shifted(h_s, state_rwkv_shift[0]).reshape(n_s, d)], axis=0)
    rkv = rwkv_rkv(h, xp, rw_mu[0, :3], rw_w_rkv[0].astype(BF16))
    lora_pad = LANES - rw_w1.shape[-1]
    pad_in = lambda w: jnp.pad(w, ((0, 0), (0, lora_pad))).astype(BF16)
    pad_out = lambda w: jnp.pad(w, ((0, lora_pad), (0, 0))).astype(BF16)
    dec, nkk, bb, k2, g, rkb = rwkv_prep(
        h, xp, rkv[0], rkv[1], rw_mu[0, 3:], pad_in(rw_w1[0]), pad_out(rw_w2[0]), pad_in(rw_a1[0]),
        pad_out(rw_a2[0]), rw_g1[0].astype(BF16), rw_g2[0].astype(BF16), rw_w0[0], rw_a0[0], rw_k_k[0],
        rw_k_a[0], rw_r_k[0].reshape(d), ones64)

    rows4 = [a.reshape(n_p + n_s, N_CHUNK, LANES) for a in (dec, nkk, bb, k2, rkv[0])]
    y_p, wkv_p = rwkv_recurrence(rows4, rkv[2], 0, bp, tp, jnp.zeros((bp, N_CHUNK, RW_HEAD, LANES), F32),
                                 ones64x4, 4, min(RW_HEAD, tp))
    y_s, wkv_s = rwkv_recurrence(rows4, rkv[2], n_p, bs, ts, _pack_rwkv_state(state_rwkv_wkv[0]),
                                 ones64x4, 4, min(RW_HEAD, ts))
    y = jnp.concatenate([y_p.reshape(n_p, d), y_s.reshape(n_s, d)], axis=0)
    x, xn = rwkv_post(y, rkv[2], rkb, g, x, rw_ln_w[0], rw_ln_b[0], rw_w_o[0].astype(BF16), ln_ffn_w[0], ones64)
    yp = peer(xn, pk_w_q[0].astype(BF16), pk_keys[0].reshape(2 * PK_HEADS, PK_NKEYS, LANES).astype(BF16),
              pk_u[0].astype(BF16), pk_v[0].astype(BF16).T)
    x, h1 = add_norm(x, yp, ln_mix_w[1], BF16)

    lb = jax.nn.softmax(hg_lb.astype(F32), axis=0)
    lb = (jnp.cumsum(lb, axis=0) - lb[0])[1]
    w_in = hg_w_in[0].astype(BF16)
    q = hgrn_in(h1, w_in[:, :d], lb, "silu")
    f = hgrn_in(h1, w_in[:, d:2 * d], lb, "forget")
    vin = hgrn_in(h1, w_in[:, 2 * d:3 * d], lb, "none")
    sgate = hgrn_in(h1, w_in[:, 3 * d:], lb, "silu")

    o_p, hg_p = hgrn_chunked(f, q, vin, jnp.zeros((bp, N_CHUNK, HG_HEAD, LANES), F32), 0, bp, tp, 16, LANES,
                             ones128)
    o_s, hg_s = hgrn_chunked(f, q, vin, jnp.swapaxes(state_hgrn[0], -1, -2), n_p, bs, ts, ts, 8 * ts, ones128)
    o = jnp.concatenate([o_p, o_s], axis=0)
    x, xn = hgrn_post(o, sgate, x, jnp.tile(hg_norm_w[0], N_CHUNK), hg_w_o[0].astype(BF16), ln_ffn_w[1], ones128)
    yp = peer(xn, pk_w_q[1].astype(BF16), pk_keys[1].reshape(2 * PK_HEADS, PK_NKEYS, LANES).astype(BF16),
              pk_u[1].astype(BF16), pk_v[1].astype(BF16).T)
    _, out = add_norm(x, yp, ln_f_w, F32)

    y_p, y_s = split(out)
    return (y_p, y_s,
            _unpack_rwkv_state(wkv_p)[None], h_p[:, -1][None], jnp.swapaxes(hg_p, -1, -2)[None],
            _unpack_rwkv_state(wkv_s)[None], h_s[:, -1][None], jnp.swapaxes(hg_s, -1, -2)[None])
```

```python
import functools
import math

import jax
import jax.numpy as jnp
from jax import lax
from jax.experimental import pallas as pl
from jax.experimental.pallas import tpu as pltpu

F32 = jnp.float32
BF16 = jnp.bfloat16

D_MODEL = 2048
LANES = 128
N_CHUNK = D_MODEL // LANES
RW_HEAD = 64
HG_HEAD = 128
RW_GN_EPS = 64e-5
RMS_EPS = 1e-6
PK_HEADS = 8
PK_NKEYS = 128
PK_TOPK = 16
PK_EXPERTS = PK_NKEYS * PK_NKEYS
INV_SQRT2 = 1.0 / math.sqrt(2.0)
LN2 = math.log(2.0)
LOG2E = 1.0 / LN2
VMEM_LIMIT = 56 * 1024 * 1024

ROW_TILE = 512
ROW_TILE_MANY = 128
ROW_TILE_HG_POST = 256
PQ_TOKEN_TILE = 256
PD_TOKEN_TILE = 512
PD_EXPERT_CHUNK = 1024
HG_CHUNK = 16
HG_HEADS_PER_STEP = 4
SCAN_SEQS = 4
SCAN_STEPS = 64


def _params(*sem, flags=None):
    return pltpu.CompilerParams(dimension_semantics=sem, vmem_limit_bytes=VMEM_LIMIT, flags=flags)


def _row_tile(n, cap):
    t = cap
    while n % t:
        t //= 2
    return t


def _dot(a, b):
    return jnp.dot(a, b, preferred_element_type=F32)


def _rms(x, w):
    return x * lax.rsqrt(jnp.mean(x * x, axis=-1, keepdims=True) + RMS_EPS) * w


def _group_sum(x, ones_ref):
    tm, d = x.shape
    nch = d // LANES
    xs = jnp.concatenate([x[:, j * LANES:(j + 1) * LANES] for j in range(nch)], axis=0)
    hi = xs.astype(BF16)
    lo = (xs - hi.astype(F32)).astype(BF16)
    w = ones_ref[...]
    r = _dot(hi, w) + _dot(lo, w)
    return jnp.concatenate([r[j * tm:(j + 1) * tm] for j in range(nch)], axis=1)


def _block_ones(group, size=LANES):
    i = jnp.arange(size) // group
    return (i[:, None] == i[None, :]).astype(BF16)


def _rms_kernel(x_ref, w_ref, o_ref):
    o_ref[...] = _rms(x_ref[...], w_ref[...]).astype(o_ref.dtype)


def rmsnorm(x, w, out_dtype=F32):
    n, d = x.shape
    tm = _row_tile(n, ROW_TILE)
    return pl.pallas_call(
        _rms_kernel,
        grid=(n // tm,),
        in_specs=[pl.BlockSpec((tm, d), lambda i: (i, 0)), pl.BlockSpec((1, d), lambda i: (0, 0))],
        out_specs=pl.BlockSpec((tm, d), lambda i: (i, 0)),
        out_shape=jax.ShapeDtypeStruct((n, d), out_dtype),
        compiler_params=_params("parallel"),
        name="rmsnorm",
    )(x, w.reshape(1, d))


def _addnorm_kernel(x_ref, y_ref, w_ref, s_ref, n_ref):
    s = x_ref[...] + y_ref[...]
    s_ref[...] = s
    n_ref[...] = _rms(s, w_ref[...]).astype(n_ref.dtype)


def add_norm(x, y, w, norm_dtype):
    n, d = x.shape
    tm = _row_tile(n, ROW_TILE)
    row = pl.BlockSpec((tm, d), lambda i: (i, 0))
    return pl.pallas_call(
        _addnorm_kernel,
        grid=(n // tm,),
        in_specs=[row, row, pl.BlockSpec((1, d), lambda i: (0, 0))],
        out_specs=[row, row],
        out_shape=[jax.ShapeDtypeStruct((n, d), F32), jax.ShapeDtypeStruct((n, d), norm_dtype)],
        compiler_params=_params("parallel"),
        name="add_norm",
    )(x, y, w.reshape(1, d))


def _rkv_kernel(h_ref, xp_ref, mu_ref, w_ref, o_ref):
    h = h_ref[...]
    xm = h + (xp_ref[...] - h) * mu_ref[0]
    o_ref[0] = _dot(xm.astype(BF16), w_ref[0])


def rwkv_rkv(h, xp, mu3, w3):
    n, d = h.shape
    tm = _row_tile(n, ROW_TILE)
    row = pl.BlockSpec((tm, d), lambda s, i: (i, 0))
    return pl.pallas_call(
        _rkv_kernel,
        grid=(3, n // tm),
        in_specs=[row, row,
                  pl.BlockSpec((1, 1, d), lambda s, i: (s, 0, 0)),
                  pl.BlockSpec((1, d, d), lambda s, i: (s, 0, 0))],
        out_specs=pl.BlockSpec((1, tm, d), lambda s, i: (s, i, 0)),
        out_shape=jax.ShapeDtypeStruct((3, n, d), F32),
        compiler_params=_params("arbitrary", "arbitrary"),
        name="rwkv_rkv",
    )(h, xp, mu3.reshape(3, 1, d), w3)


def _softplus(z):
    return jnp.maximum(z, 0.0) + jnp.log1p(jnp.exp(-jnp.abs(z)))


def _rwprep_kernel(h_ref, xp_ref, r_ref, k_ref, mu_ref, w1_ref, w2_ref, a1_ref, a2_ref, g1_ref, g2_ref,
                   w0_ref, a0_ref, kk_ref, ka_ref, rk_ref, ones_ref,
                   dec_o, nkk_o, b_o, k2_o, g_o, rkb_o):
    h = h_ref[...]
    dx = xp_ref[...] - h
    xw = (h + dx * mu_ref[0]).astype(BF16)
    xa = (h + dx * mu_ref[1]).astype(BF16)
    xg = (h + dx * mu_ref[2]).astype(BF16)
    wl = w0_ref[...] + _dot(jnp.tanh(_dot(xw, w1_ref[...])).astype(BF16), w2_ref[...])
    wv = -_softplus(-wl) - 0.5
    dec_o[...] = jnp.exp(-jnp.exp(wv))
    a = jax.nn.sigmoid(a0_ref[...] + _dot(_dot(xa, a1_ref[...]).astype(BF16), a2_ref[...]))
    g_o[...] = _dot(jax.nn.sigmoid(_dot(xg, g1_ref[...])).astype(BF16), g2_ref[...])
    k = k_ref[...]
    kk = k * kk_ref[...]
    kk = kk * lax.rsqrt(jnp.maximum(_group_sum(kk * kk, ones_ref), 1e-24))
    k2 = k * (1.0 + (a - 1.0) * ka_ref[...])
    nkk_o[...] = -kk
    b_o[...] = kk * a
    k2_o[...] = k2
    rkb_o[...] = _group_sum(r_ref[...] * k2 * rk_ref[...], ones_ref)


def rwkv_prep(h, xp, r, k, mu3, w1, w2, a1, a2, g1, g2, w0, a0, k_k, k_a, r_k, ones64):
    n, d = h.shape
    tm = _row_tile(n, ROW_TILE_MANY)
    row = pl.BlockSpec((tm, d), lambda i: (i, 0))
    vec = pl.BlockSpec((1, d), lambda i: (0, 0))

    def full(a):
        return pl.BlockSpec(a.shape, lambda i: (0,) * a.ndim)

    mu3 = mu3.reshape(3, 1, d)
    vecs = [v.reshape(1, d) for v in (w0, a0, k_k, k_a, r_k)]
    return pl.pallas_call(
        _rwprep_kernel,
        grid=(n // tm,),
        in_specs=[row, row, row, row, full(mu3), full(w1), full(w2), full(a1), full(a2), full(g1), full(g2),
                  vec, vec, vec, vec, vec, full(ones64)],
        out_specs=[row] * 6,
        out_shape=[jax.ShapeDtypeStruct((n, d), F32)] * 6,
        compiler_params=_params("parallel"),
        name="rwkv_prep",
    )(h, xp, r, k, mu3, w1, w2, a1, a2, g1, g2, *vecs, ones64)


def _rwpost_kernel(y_ref, v_ref, rkb_ref, g_ref, x_ref, lnw_ref, lnb_ref, wo_ref, nw_ref, ones_ref,
                   x1_o, xn_o):
    y = y_ref[...]
    d = y - _group_sum(y, ones_ref) * (1.0 / RW_HEAD)
    var = _group_sum(d * d, ones_ref) * (1.0 / RW_HEAD)
    z = d * lax.rsqrt(var + RW_GN_EPS) * lnw_ref[...] + lnb_ref[...] + rkb_ref[...] * v_ref[...]
    x1 = x_ref[...] + _dot((z * g_ref[...]).astype(BF16), wo_ref[...])
    x1_o[...] = x1
    xn_o[...] = _rms(x1, nw_ref[...]).astype(xn_o.dtype)


def rwkv_post(y, v, rkb, g, x, ln_w, ln_b, w_o, norm_w, ones64):
    n, d = y.shape
    tm = _row_tile(n, ROW_TILE_MANY)
    row = pl.BlockSpec((tm, d), lambda i: (i, 0))
    vec = pl.BlockSpec((1, d), lambda i: (0, 0))
    return pl.pallas_call(
        _rwpost_kernel,
        grid=(n // tm,),
        in_specs=[row, row, row, row, row, vec, vec, pl.BlockSpec((d, d), lambda i: (0, 0)), vec,
                  pl.BlockSpec((LANES, LANES), lambda i: (0, 0))],
        out_specs=[row, row],
        out_shape=[jax.ShapeDtypeStruct((n, d), F32), jax.ShapeDtypeStruct((n, d), BF16)],
        compiler_params=_params("parallel"),
        name="rwkv_post",
    )(y, v, rkb, g, x, ln_w.reshape(1, d), ln_b.reshape(1, d), w_o, norm_w.reshape(1, d), ones64)


def _hgin_kernel(kind, a_ref, w_ref, lb_ref, o_ref):
    o = _dot(a_ref[...], w_ref[...])
    if kind == "silu":
        o = o * jax.nn.sigmoid(o)
    elif kind == "forget":
        lb = lb_ref[...]
        o = lb + (1.0 - lb) * jax.nn.sigmoid(o)
    o_ref[...] = o


def hgrn_in(a, w, lb, kind):
    n, d = a.shape
    tm = _row_tile(n, ROW_TILE)
    return pl.pallas_call(
        functools.partial(_hgin_kernel, kind),
        grid=(n // tm,),
        in_specs=[pl.BlockSpec((tm, d), lambda i: (i, 0)), pl.BlockSpec((d, d), lambda i: (0, 0)),
                  pl.BlockSpec((1, d), lambda i: (0, 0))],
        out_specs=pl.BlockSpec((tm, d), lambda i: (i, 0)),
        out_shape=jax.ShapeDtypeStruct((n, d), F32),
        compiler_params=_params("parallel"),
        name="hgrn_in_" + kind,
    )(a, w, lb.reshape(1, d))


def _hgpost_kernel(o_ref, sg_ref, x_ref, nw_ref, wo_ref, fw_ref, ones_ref, x3_o, xn_o):
    o = o_ref[...]
    ms = _group_sum(o * o, ones_ref) * (1.0 / HG_HEAD)
    z = o * lax.rsqrt(ms + RMS_EPS) * nw_ref[...] * sg_ref[...]
    x3 = x_ref[...] + _dot(z.astype(BF16), wo_ref[...])
    x3_o[...] = x3
    xn_o[...] = _rms(x3, fw_ref[...]).astype(xn_o.dtype)


def hgrn_post(o, sgate, x, norm_w_tiled, w_o, ffn_w, ones128):
    n, d = o.shape
    tm = _row_tile(n, ROW_TILE_HG_POST)
    row = pl.BlockSpec((tm, d), lambda i: (i, 0))
    vec = pl.BlockSpec((1, d), lambda i: (0, 0))
    return pl.pallas_call(
        _hgpost_kernel,
        grid=(n // tm,),
        in_specs=[row, row, row, vec, pl.BlockSpec((d, d), lambda i: (0, 0)), vec,
                  pl.BlockSpec((LANES, LANES), lambda i: (0, 0))],
        out_specs=[row, row],
        out_shape=[jax.ShapeDtypeStruct((n, d), F32), jax.ShapeDtypeStruct((n, d), BF16)],
        compiler_params=_params("parallel"),
        name="hgrn_post",
    )(o, sgate, x, norm_w_tiled.reshape(1, d), w_o, ffn_w.reshape(1, d), ones128)


_NEG_BIG = -1e30


def _hgchunk_kernel(c, shared, hpg, f_ref, q_ref, v_ref, s0_ref, tril_ref, ones_ref, o_ref, st_ref, s_scr):
    tr = f_ref.shape[0]
    nseg = tr // c
    cp = max(c, 16)
    tile = pl.program_id(2) if shared else 0

    if shared:
        @pl.when(tile == 0)
        def _():
            s_scr[...] = s0_ref[...]
    else:
        s_scr[...] = s0_ref[...]

    row_in_chunk = lax.broadcasted_iota(jnp.int32, (tr, LANES), 0) & (c - 1)
    tril = tril_ref[...]

    def rows16(x):
        if cp == c:
            return x.astype(BF16)
        return jnp.concatenate([x, jnp.zeros((cp - c, x.shape[1]), F32)], axis=0).astype(BF16)

    for hh in range(hpg):
        sl = slice(hh * LANES, (hh + 1) * LANES)
        f, q, v = f_ref[:, sl], q_ref[:, sl], v_ref[:, sl]
        g = jnp.log(f)
        g_hi = g.astype(BF16)
        g_mid = (g - g_hi.astype(F32)).astype(BF16)
        g_lo = (g - g_hi.astype(F32) - g_mid.astype(F32)).astype(BF16)
        b = _dot(tril, g_hi) + _dot(tril, g_mid) + _dot(tril, g_lo)
        k = 1.0 - f
        terms = [(q * k).astype(BF16)]
        for d in range(1, c):
            arg = jnp.where(row_in_chunk >= d, b - pltpu.roll(b, d, 0), _NEG_BIG)
            terms.append((jnp.exp(arg) * q * pltpu.roll(k, d, 0)).astype(BF16))
        coef = _dot(jnp.concatenate(terms, axis=0), ones_ref[...])
        o = coef[0:tr] * v
        for d in range(1, c):
            o = o + coef[d * tr:(d + 1) * tr] * pltpu.roll(v, d, 0)
        qb = q * jnp.exp(b)
        decays, updates = [], []
        for seg in range(nseg):
            r = slice(seg * c, (seg + 1) * c)
            b_end = b[seg * c + c - 1:seg * c + c, :]
            kd = rows16(k[r] * jnp.exp(b_end - b[r]))
            decays.append(jnp.exp(b_end))
            updates.append(lax.dot_general(rows16(v[r]), kd, (((0,), (0,)), ((), ())),
                                           preferred_element_type=F32))
        starts = []
        for seg in range(nseg):
            slot = 0 if shared else seg
            s_old = s_scr[slot, hh]
            starts.append(s_old.astype(BF16))
            s_scr[slot, hh] = s_old * decays[seg] + updates[seg]
        inter = []
        for seg in range(nseg):
            oi = lax.dot_general(rows16(qb[seg * c:(seg + 1) * c]), starts[seg], (((1,), (1,)), ((), ())),
                                 preferred_element_type=F32)
            inter.append(oi[:c])
        o_ref[:, sl] = o + jnp.concatenate(inter, axis=0)

    if shared:
        @pl.when(tile == pl.num_programs(2) - 1)
        def _():
            st_ref[...] = s_scr[...]
    else:
        st_ref[...] = s_scr[...]


def hgrn_chunked(f, q, v, s0, row0, n_seq, t_len, c, tr, ones128):
    d = f.shape[1]
    hpg = HG_HEADS_PER_STEP
    ngrp = N_CHUNK // hpg
    shared = t_len >= tr
    i = jnp.arange(tr)
    tril = ((i[:, None] // c == i[None, :] // c) & (i[None, :] <= i[:, None])).astype(BF16)
    r0 = row0 // tr
    if shared:
        tiles = t_len // tr
        grid = (n_seq, ngrp, tiles)
        seq = pl.BlockSpec((tr, hpg * LANES), lambda s, g, t: (r0 + s * tiles + t, g))
        oseq = pl.BlockSpec((tr, hpg * LANES), lambda s, g, t: (s * tiles + t, g))
        state = pl.BlockSpec((1, hpg, HG_HEAD, LANES), lambda s, g, t: (s, g, 0, 0))
        const = lambda s, g, t: (0, 0)
        sem = ("parallel", "parallel", "arbitrary")
        nslot = 1
    else:
        nslot = tr // t_len
        grid = (n_seq // nslot, ngrp)
        seq = pl.BlockSpec((tr, hpg * LANES), lambda s, g: (r0 + s, g))
        oseq = pl.BlockSpec((tr, hpg * LANES), lambda s, g: (s, g))
        state = pl.BlockSpec((nslot, hpg, HG_HEAD, LANES), lambda s, g: (s, g, 0, 0))
        const = lambda s, g: (0, 0)
        sem = ("parallel", "parallel")
    return pl.pallas_call(
        functools.partial(_hgchunk_kernel, c, shared, hpg),
        grid=grid,
        in_specs=[seq, seq, seq, state, pl.BlockSpec((tr, tr), const), pl.BlockSpec((LANES, LANES), const)],
        out_specs=[oseq, state],
        out_shape=[jax.ShapeDtypeStruct((n_seq * t_len, d), F32), jax.ShapeDtypeStruct(s0.shape, F32)],
        scratch_shapes=[pltpu.VMEM((nslot, hpg, HG_HEAD, LANES), F32)],
        compiler_params=_params(*sem),
        name="hgrn_chunked",
    )(f, q, v, s0, tril, ones128)


SCAN_GROUPS = 2


def _scan_kernel(bg, tb, *refs):
    w_ref, nkk_ref, b_ref, k_ref, r_ref, v_ref = (refs[i * bg:(i + 1) * bg] for i in range(6))
    s0_ref, ones_ref, y_ref, st_ref, s_scr, vt_scr, yt_scr, l1, r1, l2, r2 = refs[6 * bg:]
    rows = RW_HEAD
    nc = bg * N_CHUNK
    ncg = nc // SCAN_GROUPS
    tblk = pl.program_id(1)
    lane = lax.broadcasted_iota(jnp.int32, (rows, LANES), 1)
    low = lane < RW_HEAD

    @pl.when(tblk == 0)
    def _():
        for c in range(nc):
            s_scr[c] = s0_ref[c // N_CHUNK, c % N_CHUNK]

    def lanes_of(c):
        return slice((c % N_CHUNK) * LANES, (c % N_CHUNK + 1) * LANES)

    def row(ref, c, t):
        return ref[c // N_CHUNK][t, c % N_CHUNK:c % N_CHUNK + 1, :]

    def pair_slot(cl):
        return slice((cl // 2) * rows, (cl // 2 + 1) * rows), slice((cl % 2) * LANES, (cl % 2 + 1) * LANES)

    for c in range(nc):
        vb = v_ref[c // N_CHUNK][:, lanes_of(c)]
        vb = jnp.concatenate([vb, jnp.zeros((LANES - tb, LANES), F32)], axis=0)
        vt = vb.T
        vt_scr[c] = jnp.where(low, vt[:RW_HEAD], pltpu.roll(vt[RW_HEAD:], RW_HEAD, 1))
        yt_scr[c] = jnp.zeros((rows, LANES), F32)

    def reduce_state(g, t, onehot):
        for cl in range(ncg):
            c = g * ncg + cl
            p = (s_scr[c] * row(nkk_ref, c, t)).astype(BF16)
            vsel = jnp.where(onehot, vt_scr[c], 0.0).astype(BF16)
            l1[g, cl * rows:(cl + 1) * rows, :] = jnp.concatenate([p, vsel], axis=1)
        r1[g] = _dot(l1[g], ones_ref[...])

    def update(g, t):
        for cl in range(ncg):
            c = g * ncg + cl
            rs, ls = pair_slot(cl)
            sa = r1[g, cl * rows:(cl + 1) * rows, 0:LANES]
            vc = r1[g, cl * rows:(cl + 1) * rows, LANES:2 * LANES]
            s_new = s_scr[c] * row(w_ref, c, t) + sa * row(b_ref, c, t) + vc * row(k_ref, c, t)
            s_scr[c] = s_new
            l2[g, rs, ls] = (s_new * row(r_ref, c, t)).astype(BF16)
        r2[g] = _dot(l2[g], ones_ref[...])

    def collect(g, onehot):
        for cl in range(ncg):
            rs, ls = pair_slot(cl)
            c = g * ncg + cl
            yt_scr[c] = jnp.where(onehot, r2[g, rs, ls], yt_scr[c])

    def step(t, carry):
        onehot = (lane & (RW_HEAD - 1)) == t
        for g in range(SCAN_GROUPS):
            reduce_state(g, t, onehot)
        for g in range(SCAN_GROUPS):
            update(g, t)
        for g in range(SCAN_GROUPS):
            collect(g, onehot)
        return carry

    lax.fori_loop(0, tb, step, 0, unroll=2)

    for c in range(nc):
        y0 = yt_scr[c]
        yt = jnp.concatenate([jnp.where(low, y0, 0.0), jnp.where(low, pltpu.roll(y0, RW_HEAD, 1), 0.0)], axis=0)
        y_ref[c // N_CHUNK, :, lanes_of(c)] = yt.T[:tb]

    @pl.when(tblk == pl.num_programs(1) - 1)
    def _():
        for c in range(nc):
            st_ref[c // N_CHUNK, c % N_CHUNK] = s_scr[c]


def rwkv_recurrence(row_inputs, v, row0, b, t, s0, ones, bg, tb):
    d = v.shape[1]
    assert tb <= RW_HEAD and t % tb == 0 and b % bg == 0 and row0 % tb == 0
    nc = bg * N_CHUNK
    ncg = nc // SCAN_GROUPS
    tiles = t // tb

    def member(block, m):
        return pl.BlockSpec(block, lambda i, j: (row0 // tb + (i * bg + m) * tiles + j,) + (0,) * (len(block) - 1))

    seq = pl.BlockSpec((bg, tb, d), lambda i, j: (i, j, 0))
    state = pl.BlockSpec((bg, N_CHUNK, RW_HEAD, LANES), lambda i, j: (i, 0, 0, 0))
    operands, specs = [], []
    for a in row_inputs:
        operands += [a] * bg
        specs += [member((tb, N_CHUNK, LANES), m) for m in range(bg)]
    operands += [v] * bg
    specs += [member((tb, d), m) for m in range(bg)]
    return pl.pallas_call(
        functools.partial(_scan_kernel, bg, tb),
        grid=(b // bg, tiles),
        in_specs=specs + [state, pl.BlockSpec((2 * LANES, 2 * LANES), lambda i, j: (0, 0))],
        out_specs=[seq, state],
        out_shape=[jax.ShapeDtypeStruct((b, t, d), F32), jax.ShapeDtypeStruct(s0.shape, F32)],
        scratch_shapes=[pltpu.VMEM((nc, RW_HEAD, LANES), F32),
                        pltpu.VMEM((nc, RW_HEAD, LANES), F32),
                        pltpu.VMEM((nc, RW_HEAD, LANES), F32),
                        pltpu.VMEM((SCAN_GROUPS, ncg * RW_HEAD, 2 * LANES), BF16),
                        pltpu.VMEM((SCAN_GROUPS, ncg * RW_HEAD, 2 * LANES), F32),
                        pltpu.VMEM((SCAN_GROUPS, ncg * RW_HEAD // 2, 2 * LANES), BF16),
                        pltpu.VMEM((SCAN_GROUPS, ncg * RW_HEAD // 2, 2 * LANES), F32)],
        compiler_params=_params("parallel", "arbitrary"),
        name="scan_rwkv",
    )(*operands, s0, ones)


SUBLANES = 8


def _sort_network(n):
    pairs, p = [], 1
    while p < n:
        k = p
        while k >= 1:
            for j in range(k % p, n - k, 2 * k):
                for i in range(min(k, n - j - k)):
                    if (i + j) // (2 * p) == (i + j + k) // (2 * p):
                        pairs.append((i + j, i + j + k))
            k //= 2
        p *= 2
    return pairs


def _exchange(v, i, j):
    v[i], v[j] = jnp.maximum(v[i], v[j]), jnp.minimum(v[i], v[j])


def _bitonic_merge(v):
    k = len(v) // 2
    while k >= 1:
        for i in range(len(v)):
            if not i & k:
                _exchange(v, i, i + k)
        k //= 2


def _merge_sublanes(v):
    for shift in (4, 2, 1):
        w = [pltpu.roll(x, shift, 0) for x in v]
        if len(v) < PK_TOPK:
            v = v + w[::-1]
        else:
            v = [jnp.maximum(v[i], w[PK_TOPK - 1 - i]) for i in range(PK_TOPK)]
        _bitonic_merge(v)
    return v


def _top16_of_rows(s):
    v = [s[SUBLANES * i:SUBLANES * (i + 1), :] for i in range(s.shape[0] // SUBLANES)]
    for i, j in _sort_network(len(v)):
        _exchange(v, i, j)
    return _merge_sublanes(v)


_CAND_PAIRS = [(a, b) for a in range(PK_TOPK) for b in range(PK_TOPK) if (a + 1) * (b + 1) <= PK_TOPK]


def _top16_pair_sums(t1, t2):
    sub = lax.broadcasted_iota(jnp.int32, t1[0].shape, 0)
    packed = []
    for k in range(0, len(_CAND_PAIRS), SUBLANES):
        tile = jnp.full(t1[0].shape, -jnp.inf, F32)
        for r, (a, b) in enumerate(_CAND_PAIRS[k:k + SUBLANES]):
            tile = jnp.where(sub == r, t1[a] + t2[b], tile)
        packed.append(tile)
    while len(packed) & (len(packed) - 1):
        packed.append(jnp.full(t1[0].shape, -jnp.inf, F32))
    for i, j in _sort_network(len(packed)):
        _exchange(packed, i, j)
    return _merge_sublanes(packed)


def _pquery_kernel(xn_ref, wq_ref, keys_ref, st_ref, thr_ref):
    q = _dot(xn_ref[...], wq_ref[...])
    for c in range(2 * PK_HEADS):
        qc = q[:, c * LANES:(c + 1) * LANES].astype(BF16)
        st_ref[c * LANES:(c + 1) * LANES, :] = lax.dot_general(
            keys_ref[c], qc, (((1,), (1,)), ((), ())), preferred_element_type=F32)

    for h in range(PK_HEADS):
        r1 = slice((2 * h) * LANES, (2 * h + 1) * LANES)
        r2 = slice((2 * h + 1) * LANES, (2 * h + 2) * LANES)
        s1, s2 = st_ref[r1, :], st_ref[r2, :]
        t1 = _top16_of_rows(s1)
        t2 = _top16_of_rows(s2)
        top = _top16_pair_sums(t1, t2)
        z = jnp.zeros_like(top[0])
        for m in top:
            z = z + jnp.exp(m - top[0])
        off = top[0] + jnp.log(z) + LN2
        st_ref[r1, :] = (s1 - off[0:1, :]) * LOG2E
        st_ref[r2, :] = s2 * LOG2E
        top = _top16_pair_sums([(t - off) * LOG2E for t in t1], [t * LOG2E for t in t2])
        thr_ref[h:h + 1, :] = top[-1][0:1, :]


def peer_query(xn, w_q, keys):
    n, d = xn.shape
    tm = _row_tile(n, PQ_TOKEN_TILE)
    return pl.pallas_call(
        _pquery_kernel,
        grid=(n // tm,),
        in_specs=[pl.BlockSpec((tm, d), lambda i: (i, 0)), pl.BlockSpec((d, d), lambda i: (0, 0)),
                  pl.BlockSpec(keys.shape, lambda i: (0, 0, 0))],
        out_specs=[pl.BlockSpec((d, tm), lambda i: (0, i)), pl.BlockSpec((PK_HEADS, tm), lambda i: (0, i))],
        out_shape=[jax.ShapeDtypeStruct((d, n), F32), jax.ShapeDtypeStruct((PK_HEADS, n), F32)],
        compiler_params=_params("parallel"),
        name="peer_query",
    )(xn, w_q, keys)


PD_PARTS = 4


def _pdense_kernel(ec, xn_ref, u_ref, vt_ref, st_ref, thr_ref, y_ref, acc, xt_scr, hid_scr):
    j = pl.program_id(1)
    groups = ec // PK_NKEYS
    sub = 8

    @pl.when(j == 0)
    def _():
        acc[...] = jnp.zeros(acc.shape, F32)
        xt_scr[...] = xn_ref[...].astype(F32).T.astype(BF16)

    tm = xn_ref.shape[0]
    gpp = groups // PD_PARTS
    rows = gpp * PK_NKEYS
    thr = [jnp.broadcast_to(thr_ref[h:h + 1, :], (sub, tm)) for h in range(PK_HEADS)]

    def hidden(part):
        r = slice(part * rows, (part + 1) * rows)
        hid_scr[r, :] = _dot(u_ref[r, :], xt_scr[...])

    def finish(part):
        ps = []
        for ii in range(part * gpp, (part + 1) * gpp):
            gates = [None] * (PK_NKEYS // sub)
            for h in range(PK_HEADS):
                s1 = jnp.broadcast_to(st_ref[pl.ds(2 * h * LANES + j * groups + ii, 1), :], (sub, tm))
                for s in range(PK_NKEYS // sub):
                    c = s1 + st_ref[(2 * h + 1) * LANES + s * sub:(2 * h + 1) * LANES + (s + 1) * sub, :]
                    e = jnp.where(c >= thr[h], jnp.exp2(c), 0.0)
                    gates[s] = e if gates[s] is None else gates[s] + e
            hh = hid_scr[ii * PK_NKEYS:(ii + 1) * PK_NKEYS, :]
            ps.append((hh * (1.0 + lax.erf(hh * INV_SQRT2)) * jnp.concatenate(gates, axis=0)).astype(BF16))
        acc[...] += _dot(vt_ref[:, part * rows:(part + 1) * rows], jnp.concatenate(ps, axis=0))

    hidden(0)
    for part in range(PD_PARTS):
        if part + 1 < PD_PARTS:
            hidden(part + 1)
        finish(part)

    @pl.when(j == pl.num_programs(1) - 1)
    def _():
        y_ref[...] = acc[...].T


def peer_dense(xn, u, v_t, st, thr):
    n, d = xn.shape
    tm = _row_tile(n, PD_TOKEN_TILE)
    ec = PD_EXPERT_CHUNK
    return pl.pallas_call(
        functools.partial(_pdense_kernel, ec),
        grid=(n // tm, PK_EXPERTS // ec),
        in_specs=[pl.BlockSpec((tm, d), lambda i, j: (i, 0)),
                  pl.BlockSpec((ec, d), lambda i, j: (j, 0)),
                  pl.BlockSpec((d, ec), lambda i, j: (0, j)),
                  pl.BlockSpec((d, tm), lambda i, j: (0, i)),
                  pl.BlockSpec((PK_HEADS, tm), lambda i, j: (0, i))],
        out_specs=pl.BlockSpec((tm, d), lambda i, j: (i, 0)),
        out_shape=jax.ShapeDtypeStruct((n, d), F32),
        scratch_shapes=[pltpu.VMEM((d, tm), F32), pltpu.VMEM((d, tm), BF16), pltpu.VMEM((ec, tm), F32)],
        compiler_params=_params("parallel", "arbitrary"),
        name="peer_dense",
    )(xn, u, v_t, st, thr)


def peer(xn, w_q, keys, u, v_t):
    st, thr = peer_query(xn, w_q, keys)
    return peer_dense(xn, u, v_t, st, thr)


def _pack_rwkv_state(s):
    b = s.shape[0]
    return s.reshape(b, N_CHUNK, 2, RW_HEAD, RW_HEAD).transpose(0, 1, 3, 2, 4).reshape(b, N_CHUNK, RW_HEAD, LANES)


def _unpack_rwkv_state(s):
    b = s.shape[0]
    return s.reshape(b, N_CHUNK, RW_HEAD, 2, RW_HEAD).transpose(0, 1, 3, 2, 4).reshape(b, 2 * N_CHUNK, RW_HEAD, RW_HEAD)


def _shifted(h, shift0):
    return jnp.concatenate([shift0[:, None].astype(h.dtype), h[:, :-1]], axis=1)


def kernel(x_prompt, x_sample, state_rwkv_wkv, state_rwkv_shift, state_hgrn, ln_mix_w, ln_ffn_w, ln_f_w, rw_mu, rw_w_rkv, rw_w0, rw_w1, rw_w2, rw_a0, rw_a1, rw_a2, rw_g1, rw_g2, rw_k_k, rw_k_a, rw_r_k, rw_ln_w, rw_ln_b, rw_w_o, hg_w_in, hg_lb, hg_norm_w, hg_w_o, pk_w_q, pk_keys, pk_u, pk_v):
    bp, tp, d = x_prompt.shape
    bs, ts, _ = x_sample.shape
    n_p, n_s = bp * tp, bs * ts
    ones64 = _block_ones(RW_HEAD)
    ones128 = _block_ones(HG_HEAD)
    ones64x4 = _block_ones(RW_HEAD, 2 * LANES)

    def split(a):
        return a[:n_p].reshape(bp, tp, d), a[n_p:].reshape(bs, ts, d)

    x = jnp.concatenate([x_prompt.reshape(n_p, d), x_sample.reshape(n_s, d)], axis=0)

    h = rmsnorm(x, ln_mix_w[0])
    h_p, h_s = split(h)
    xp = jnp.concatenate([_shifted(h_p, jnp.zeros((bp, d), F32)).reshape(n_p, d),
                          _shifted(h_s, state_rwkv_shift[0]).reshape(n_s, d)], axis=0)
    rkv = rwkv_rkv(h, xp, rw_mu[0, :3], rw_w_rkv[0].astype(BF16))
    lora_pad = LANES - rw_w1.shape[-1]
    pad_in = lambda w: jnp.pad(w, ((0, 0), (0, lora_pad))).astype(BF16)
    pad_out = lambda w: jnp.pad(w, ((0, lora_pad), (0, 0))).astype(BF16)
    dec, nkk, bb, k2, g, rkb = rwkv_prep(
        h, xp, rkv[0], rkv[1], rw_mu[0, 3:], pad_in(rw_w1[0]), pad_out(rw_w2[0]), pad_in(rw_a1[0]),
        pad_out(rw_a2[0]), rw_g1[0].astype(BF16), rw_g2[0].astype(BF16), rw_w0[0], rw_a0[0], rw_k_k[0],
        rw_k_a[0], rw_r_k[0].reshape(d), ones64)

    rows4 = [a.reshape(n_p + n_s, N_CHUNK, LANES) for a in (dec, nkk, bb, k2, rkv[0])]
    y_p, wkv_p = rwkv_recurrence(rows4, rkv[2], 0, bp, tp, jnp.zeros((bp, N_CHUNK, RW_HEAD, LANES), F32),
                                 ones64x4, SCAN_SEQS, min(SCAN_STEPS, tp))
    y_s, wkv_s = rwkv_recurrence(rows4, rkv[2], n_p, bs, ts, _pack_rwkv_state(state_rwkv_wkv[0]),
                                 ones64x4, SCAN_SEQS, min(SCAN_STEPS, ts))
    y = jnp.concatenate([y_p.reshape(n_p, d), y_s.reshape(n_s, d)], axis=0)
    x, xn = rwkv_post(y, rkv[2], rkb, g, x, rw_ln_w[0], rw_ln_b[0], rw_w_o[0].astype(BF16), ln_ffn_w[0], ones64)
    yp = peer(xn, pk_w_q[0].astype(BF16), pk_keys[0].reshape(2 * PK_HEADS, PK_NKEYS, LANES).astype(BF16),
              pk_u[0].astype(BF16), pk_v[0].astype(BF16).T)
    x, h1 = add_norm(x, yp, ln_mix_w[1], BF16)

    lb = jax.nn.softmax(hg_lb.astype(F32), axis=0)
    lb = (jnp.cumsum(lb, axis=0) - lb[0])[1]
    w_in = hg_w_in[0].astype(BF16)
    q = hgrn_in(h1, w_in[:, :d], lb, "silu")
    f = hgrn_in(h1, w_in[:, d:2 * d], lb, "forget")
    vin = hgrn_in(h1, w_in[:, 2 * d:3 * d], lb, "none")
    sgate = hgrn_in(h1, w_in[:, 3 * d:], lb, "silu")

    o_p, hg_p = hgrn_chunked(f, q, vin, jnp.zeros((bp, N_CHUNK, HG_HEAD, LANES), F32), 0, bp, tp, HG_CHUNK, LANES,
                             ones128)
    o_s, hg_s = hgrn_chunked(f, q, vin, jnp.swapaxes(state_hgrn[0], -1, -2), n_p, bs, ts, ts, 8 * ts, ones128)
    o = jnp.concatenate([o_p, o_s], axis=0)
    x, xn = hgrn_post(o, sgate, x, jnp.tile(hg_norm_w[0], N_CHUNK), hg_w_o[0].astype(BF16), ln_ffn_w[1], ones128)
    yp = peer(xn, pk_w_q[1].astype(BF16), pk_keys[1].reshape(2 * PK_HEADS, PK_NKEYS, LANES).astype(BF16),
              pk_u[1].astype(BF16), pk_v[1].astype(BF16).T)
    _, out = add_norm(x, yp, ln_f_w, F32)

    y_p, y_s = split(out)
    return (y_p, y_s,
            _unpack_rwkv_state(wkv_p)[None], h_p[:, -1][None], jnp.swapaxes(hg_p, -1, -2)[None],
            _unpack_rwkv_state(wkv_s)[None], h_s[:, -1][None], jnp.swapaxes(hg_s, -1, -2)[None])
```

```python
import functools
import math

import jax
import jax.numpy as jnp
from jax import lax
from jax.experimental import pallas as pl
from jax.experimental.pallas import tpu as pltpu

F32 = jnp.float32
BF16 = jnp.bfloat16

D_MODEL = 2048
LANES = 128
N_CHUNK = D_MODEL // LANES
RW_HEAD = 64
HG_HEAD = 128
RW_GN_EPS = 64e-5
RMS_EPS = 1e-6
PK_HEADS = 8
PK_NKEYS = 128
PK_TOPK = 16
PK_EXPERTS = PK_NKEYS * PK_NKEYS
INV_SQRT2 = 1.0 / math.sqrt(2.0)
LN2 = math.log(2.0)
LOG2E = 1.0 / LN2
VMEM_LIMIT = 56 * 1024 * 1024

ROW_TILE = 512
ROW_TILE_MANY = 128
ROW_TILE_HG_POST = 256
PQ_TOKEN_TILE = 256
PD_TOKEN_TILE = 512
PD_EXPERT_CHUNK = 1024
HG_CHUNK = 16
HG_HEADS_PER_STEP = 4
SCAN_SEQS = 4
SCAN_STEPS = 64


def _params(*sem, flags=None):
    return pltpu.CompilerParams(dimension_semantics=sem, vmem_limit_bytes=VMEM_LIMIT, flags=flags)


def _row_tile(n, cap):
    t = cap
    while n % t:
        t //= 2
    return t


def _dot(a, b):
    return jnp.dot(a, b, preferred_element_type=F32)


def _rms(x, w):
    return x * lax.rsqrt(jnp.mean(x * x, axis=-1, keepdims=True) + RMS_EPS) * w


def _group_sum(x, ones_ref):
    tm, d = x.shape
    nch = d // LANES
    xs = jnp.concatenate([x[:, j * LANES:(j + 1) * LANES] for j in range(nch)], axis=0)
    hi = xs.astype(BF16)
    lo = (xs - hi.astype(F32)).astype(BF16)
    w = ones_ref[...]
    r = _dot(hi, w) + _dot(lo, w)
    return jnp.concatenate([r[j * tm:(j + 1) * tm] for j in range(nch)], axis=1)


def _block_ones(group, size=LANES):
    i = jnp.arange(size) // group
    return (i[:, None] == i[None, :]).astype(BF16)


def _rms_kernel(x_ref, w_ref, o_ref):
    o_ref[...] = _rms(x_ref[...], w_ref[...]).astype(o_ref.dtype)


def rmsnorm(x, w, out_dtype=F32):
    n, d = x.shape
    tm = _row_tile(n, ROW_TILE)
    return pl.pallas_call(
        _rms_kernel,
        grid=(n // tm,),
        in_specs=[pl.BlockSpec((tm, d), lambda i: (i, 0)), pl.BlockSpec((1, d), lambda i: (0, 0))],
        out_specs=pl.BlockSpec((tm, d), lambda i: (i, 0)),
        out_shape=jax.ShapeDtypeStruct((n, d), out_dtype),
        compiler_params=_params("parallel"),
        name="rmsnorm",
    )(x, w.reshape(1, d))


def _addnorm_kernel(x_ref, y_ref, w_ref, s_ref, n_ref):
    s = x_ref[...] + y_ref[...]
    s_ref[...] = s
    n_ref[...] = _rms(s, w_ref[...]).astype(n_ref.dtype)


def add_norm(x, y, w, norm_dtype):
    n, d = x.shape
    tm = _row_tile(n, ROW_TILE)
    row = pl.BlockSpec((tm, d), lambda i: (i, 0))
    return pl.pallas_call(
        _addnorm_kernel,
        grid=(n // tm,),
        in_specs=[row, row, pl.BlockSpec((1, d), lambda i: (0, 0))],
        out_specs=[row, row],
        out_shape=[jax.ShapeDtypeStruct((n, d), F32), jax.ShapeDtypeStruct((n, d), norm_dtype)],
        compiler_params=_params("parallel"),
        name="add_norm",
    )(x, y, w.reshape(1, d))


def _rkv_kernel(h_ref, xp_ref, mu_ref, w_ref, o_ref):
    h = h_ref[...]
    xm = h + (xp_ref[...] - h) * mu_ref[0]
    o_ref[0] = _dot(xm.astype(BF16), w_ref[0])


def rwkv_rkv(h, xp, mu3, w3):
    n, d = h.shape
    tm = _row_tile(n, ROW_TILE)
    row = pl.BlockSpec((tm, d), lambda s, i: (i, 0))
    return pl.pallas_call(
        _rkv_kernel,
        grid=(3, n // tm),
        in_specs=[row, row,
                  pl.BlockSpec((1, 1, d), lambda s, i: (s, 0, 0)),
                  pl.BlockSpec((1, d, d), lambda s, i: (s, 0, 0))],
        out_specs=pl.BlockSpec((1, tm, d), lambda s, i: (s, i, 0)),
        out_shape=jax.ShapeDtypeStruct((3, n, d), F32),
        compiler_params=_params("arbitrary", "arbitrary"),
        name="rwkv_rkv",
    )(h, xp, mu3.reshape(3, 1, d), w3)


def _softplus(z):
    return jnp.maximum(z, 0.0) + jnp.log1p(jnp.exp(-jnp.abs(z)))


def _rwprep_kernel(h_ref, xp_ref, r_ref, k_ref, mu_ref, w1_ref, w2_ref, a1_ref, a2_ref, g1_ref, g2_ref,
                   w0_ref, a0_ref, kk_ref, ka_ref, rk_ref, ones_ref,
                   dec_o, nkk_o, b_o, k2_o, g_o, rkb_o):
    h = h_ref[...]
    dx = xp_ref[...] - h
    xw = (h + dx * mu_ref[0]).astype(BF16)
    xa = (h + dx * mu_ref[1]).astype(BF16)
    xg = (h + dx * mu_ref[2]).astype(BF16)
    wl = w0_ref[...] + _dot(jnp.tanh(_dot(xw, w1_ref[...])).astype(BF16), w2_ref[...])
    wv = -_softplus(-wl) - 0.5
    dec_o[...] = jnp.exp(-jnp.exp(wv))
    a = jax.nn.sigmoid(a0_ref[...] + _dot(_dot(xa, a1_ref[...]).astype(BF16), a2_ref[...]))
    g_o[...] = _dot(jax.nn.sigmoid(_dot(xg, g1_ref[...])).astype(BF16), g2_ref[...])
    k = k_ref[...]
    kk = k * kk_ref[...]
    kk = kk * lax.rsqrt(jnp.maximum(_group_sum(kk * kk, ones_ref), 1e-24))
    k2 = k * (1.0 + (a - 1.0) * ka_ref[...])
    nkk_o[...] = -kk
    b_o[...] = kk * a
    k2_o[...] = k2
    rkb_o[...] = _group_sum(r_ref[...] * k2 * rk_ref[...], ones_ref)


def rwkv_prep(h, xp, r, k, mu3, w1, w2, a1, a2, g1, g2, w0, a0, k_k, k_a, r_k, ones64):
    n, d = h.shape
    tm = _row_tile(n, ROW_TILE_MANY)
    row = pl.BlockSpec((tm, d), lambda i: (i, 0))
    vec = pl.BlockSpec((1, d), lambda i: (0, 0))

    def full(a):
        return pl.BlockSpec(a.shape, lambda i: (0,) * a.ndim)

    mu3 = mu3.reshape(3, 1, d)
    vecs = [v.reshape(1, d) for v in (w0, a0, k_k, k_a, r_k)]
    return pl.pallas_call(
        _rwprep_kernel,
        grid=(n // tm,),
        in_specs=[row, row, row, row, full(mu3), full(w1), full(w2), full(a1), full(a2), full(g1), full(g2),
                  vec, vec, vec, vec, vec, full(ones64)],
        out_specs=[row] * 6,
        out_shape=[jax.ShapeDtypeStruct((n, d), F32)] * 6,
        compiler_params=_params("parallel"),
        name="rwkv_prep",
    )(h, xp, r, k, mu3, w1, w2, a1, a2, g1, g2, *vecs, ones64)


def _rwpost_kernel(y_ref, v_ref, rkb_ref, g_ref, x_ref, lnw_ref, lnb_ref, wo_ref, nw_ref, ones_ref,
                   x1_o, xn_o):
    y = y_ref[...]
    d = y - _group_sum(y, ones_ref) * (1.0 / RW_HEAD)
    var = _group_sum(d * d, ones_ref) * (1.0 / RW_HEAD)
    z = d * lax.rsqrt(var + RW_GN_EPS) * lnw_ref[...] + lnb_ref[...] + rkb_ref[...] * v_ref[...]
    x1 = x_ref[...] + _dot((z * g_ref[...]).astype(BF16), wo_ref[...])
    x1_o[...] = x1
    xn_o[...] = _rms(x1, nw_ref[...]).astype(xn_o.dtype)


def rwkv_post(y, v, rkb, g, x, ln_w, ln_b, w_o, norm_w, ones64):
    n, d = y.shape
    tm = _row_tile(n, ROW_TILE_MANY)
    row = pl.BlockSpec((tm, d), lambda i: (i, 0))
    vec = pl.BlockSpec((1, d), lambda i: (0, 0))
    return pl.pallas_call(
        _rwpost_kernel,
        grid=(n // tm,),
        in_specs=[row, row, row, row, row, vec, vec, pl.BlockSpec((d, d), lambda i: (0, 0)), vec,
                  pl.BlockSpec((LANES, LANES), lambda i: (0, 0))],
        out_specs=[row, row],
        out_shape=[jax.ShapeDtypeStruct((n, d), F32), jax.ShapeDtypeStruct((n, d), BF16)],
        compiler_params=_params("parallel"),
        name="rwkv_post",
    )(y, v, rkb, g, x, ln_w.reshape(1, d), ln_b.reshape(1, d), w_o, norm_w.reshape(1, d), ones64)


def _hgin_kernel(a_ref, w_ref, lb_ref, o_ref):
    s = pl.program_id(0)
    o = _dot(a_ref[...], w_ref[...])
    sig = jax.nn.sigmoid(o)
    lb = lb_ref[...]
    o_ref[0] = jnp.where(s == 2, o, jnp.where(s == 1, lb + (1.0 - lb) * sig, o * sig))


def hgrn_in(a, w_in, lb):
    n, d = a.shape
    tm = _row_tile(n, ROW_TILE)
    return pl.pallas_call(
        _hgin_kernel,
        grid=(4, n // tm),
        in_specs=[pl.BlockSpec((tm, d), lambda s, i: (i, 0)), pl.BlockSpec((d, d), lambda s, i: (0, s)),
                  pl.BlockSpec((1, d), lambda s, i: (0, 0))],
        out_specs=pl.BlockSpec((1, tm, d), lambda s, i: (s, i, 0)),
        out_shape=jax.ShapeDtypeStruct((4, n, d), F32),
        compiler_params=_params("arbitrary", "arbitrary"),
        name="hgrn_in",
    )(a, w_in, lb.reshape(1, d))


def _hgpost_kernel(o_ref, sg_ref, x_ref, nw_ref, wo_ref, fw_ref, ones_ref, x3_o, xn_o):
    o = o_ref[...]
    ms = _group_sum(o * o, ones_ref) * (1.0 / HG_HEAD)
    z = o * lax.rsqrt(ms + RMS_EPS) * nw_ref[...] * sg_ref[...]
    x3 = x_ref[...] + _dot(z.astype(BF16), wo_ref[...])
    x3_o[...] = x3
    xn_o[...] = _rms(x3, fw_ref[...]).astype(xn_o.dtype)


def hgrn_post(o, sgate, x, norm_w_tiled, w_o, ffn_w, ones128):
    n, d = o.shape
    tm = _row_tile(n, ROW_TILE_HG_POST)
    row = pl.BlockSpec((tm, d), lambda i: (i, 0))
    vec = pl.BlockSpec((1, d), lambda i: (0, 0))
    return pl.pallas_call(
        _hgpost_kernel,
        grid=(n // tm,),
        in_specs=[row, row, row, vec, pl.BlockSpec((d, d), lambda i: (0, 0)), vec,
                  pl.BlockSpec((LANES, LANES), lambda i: (0, 0))],
        out_specs=[row, row],
        out_shape=[jax.ShapeDtypeStruct((n, d), F32), jax.ShapeDtypeStruct((n, d), BF16)],
        compiler_params=_params("parallel"),
        name="hgrn_post",
    )(o, sgate, x, norm_w_tiled.reshape(1, d), w_o, ffn_w.reshape(1, d), ones128)


_NEG_BIG = -1e30


def _hgchunk_kernel(c, shared, hpg, f_ref, q_ref, v_ref, s0_ref, tril_ref, ones_ref, o_ref, st_ref, s_scr):
    tr = f_ref.shape[0]
    nseg = tr // c
    cp = max(c, 16)
    tile = pl.program_id(2) if shared else 0

    if shared:
        @pl.when(tile == 0)
        def _():
            s_scr[...] = s0_ref[...]
    else:
        s_scr[...] = s0_ref[...]

    row_in_chunk = lax.broadcasted_iota(jnp.int32, (tr, LANES), 0) & (c - 1)
    tril = tril_ref[...]

    def rows16(x):
        if cp == c:
            return x.astype(BF16)
        return jnp.concatenate([x, jnp.zeros((cp - c, x.shape[1]), F32)], axis=0).astype(BF16)

    for hh in range(hpg):
        sl = slice(hh * LANES, (hh + 1) * LANES)
        f, q, v = f_ref[:, sl], q_ref[:, sl], v_ref[:, sl]
        g = jnp.log(f)
        g_hi = g.astype(BF16)
        g_mid = (g - g_hi.astype(F32)).astype(BF16)
        g_lo = (g - g_hi.astype(F32) - g_mid.astype(F32)).astype(BF16)
        b = _dot(tril, g_hi) + _dot(tril, g_mid) + _dot(tril, g_lo)
        k = 1.0 - f
        terms = [(q * k).astype(BF16)]
        for d in range(1, c):
            arg = jnp.where(row_in_chunk >= d, b - pltpu.roll(b, d, 0), _NEG_BIG)
            terms.append((jnp.exp(arg) * q * pltpu.roll(k, d, 0)).astype(BF16))
        coef = _dot(jnp.concatenate(terms, axis=0), ones_ref[...])
        o = coef[0:tr] * v
        for d in range(1, c):
            o = o + coef[d * tr:(d + 1) * tr] * pltpu.roll(v, d, 0)
        qb = q * jnp.exp(b)
        decays, updates = [], []
        for seg in range(nseg):
            r = slice(seg * c, (seg + 1) * c)
            b_end = b[seg * c + c - 1:seg * c + c, :]
            kd = rows16(k[r] * jnp.exp(b_end - b[r]))
            decays.append(jnp.exp(b_end))
            updates.append(lax.dot_general(rows16(v[r]), kd, (((0,), (0,)), ((), ())),
                                           preferred_element_type=F32))
        starts = []
        for seg in range(nseg):
            slot = 0 if shared else seg
            s_old = s_scr[slot, hh]
            starts.append(s_old.astype(BF16))
            s_scr[slot, hh] = s_old * decays[seg] + updates[seg]
        inter = []
        for seg in range(nseg):
            oi = lax.dot_general(rows16(qb[seg * c:(seg + 1) * c]), starts[seg], (((1,), (1,)), ((), ())),
                                 preferred_element_type=F32)
            inter.append(oi[:c])
        o_ref[:, sl] = o + jnp.concatenate(inter, axis=0)

    if shared:
        @pl.when(tile == pl.num_programs(2) - 1)
        def _():
            st_ref[...] = s_scr[...]
    else:
        st_ref[...] = s_scr[...]


def hgrn_chunked(f, q, v, s0, row0, n_seq, t_len, c, tr, ones128):
    d = f.shape[1]
    hpg = HG_HEADS_PER_STEP
    ngrp = N_CHUNK // hpg
    shared = t_len >= tr
    i = jnp.arange(tr)
    tril = ((i[:, None] // c == i[None, :] // c) & (i[None, :] <= i[:, None])).astype(BF16)
    r0 = row0 // tr
    if shared:
        tiles = t_len // tr
        grid = (n_seq, ngrp, tiles)
        seq = pl.BlockSpec((tr, hpg * LANES), lambda s, g, t: (r0 + s * tiles + t, g))
        oseq = pl.BlockSpec((tr, hpg * LANES), lambda s, g, t: (s * tiles + t, g))
        state = pl.BlockSpec((1, hpg, HG_HEAD, LANES), lambda s, g, t: (s, g, 0, 0))
        const = lambda s, g, t: (0, 0)
        sem = ("parallel", "parallel", "arbitrary")
        nslot = 1
    else:
        nslot = tr // t_len
        grid = (n_seq // nslot, ngrp)
        seq = pl.BlockSpec((tr, hpg * LANES), lambda s, g: (r0 + s, g))
        oseq = pl.BlockSpec((tr, hpg * LANES), lambda s, g: (s, g))
        state = pl.BlockSpec((nslot, hpg, HG_HEAD, LANES), lambda s, g: (s, g, 0, 0))
        const = lambda s, g: (0, 0)
        sem = ("parallel", "parallel")
    return pl.pallas_call(
        functools.partial(_hgchunk_kernel, c, shared, hpg),
        grid=grid,
        in_specs=[seq, seq, seq, state, pl.BlockSpec((tr, tr), const), pl.BlockSpec((LANES, LANES), const)],
        out_specs=[oseq, state],
        out_shape=[jax.ShapeDtypeStruct((n_seq * t_len, d), F32), jax.ShapeDtypeStruct(s0.shape, F32)],
        scratch_shapes=[pltpu.VMEM((nslot, hpg, HG_HEAD, LANES), F32)],
        compiler_params=_params(*sem),
        name="hgrn_chunked",
    )(f, q, v, s0, tril, ones128)


SCAN_GROUPS = 2


def _scan_kernel(bg, tb, *refs):
    w_ref, nkk_ref, b_ref, k_ref, r_ref, v_ref = (refs[i * bg:(i + 1) * bg] for i in range(6))
    s0_ref, ones_ref, y_ref, st_ref, s_scr, vt_scr, yt_scr, l1, r1, l2, r2 = refs[6 * bg:]
    rows = RW_HEAD
    nc = bg * N_CHUNK
    ncg = nc // SCAN_GROUPS
    tblk = pl.program_id(1)
    lane = lax.broadcasted_iota(jnp.int32, (rows, LANES), 1)
    low = lane < RW_HEAD

    @pl.when(tblk == 0)
    def _():
        for c in range(nc):
            s_scr[c] = s0_ref[c // N_CHUNK, c % N_CHUNK]

    def lanes_of(c):
        return slice((c % N_CHUNK) * LANES, (c % N_CHUNK + 1) * LANES)

    def row(ref, c, t):
        return ref[c // N_CHUNK][t, c % N_CHUNK:c % N_CHUNK + 1, :]

    def pair_slot(cl):
        return slice((cl // 2) * rows, (cl // 2 + 1) * rows), slice((cl % 2) * LANES, (cl % 2 + 1) * LANES)

    for c in range(nc):
        vb = v_ref[c // N_CHUNK][:, lanes_of(c)]
        vb = jnp.concatenate([vb, jnp.zeros((LANES - tb, LANES), F32)], axis=0)
        vt = vb.T
        vt_scr[c] = jnp.where(low, vt[:RW_HEAD], pltpu.roll(vt[RW_HEAD:], RW_HEAD, 1))
        yt_scr[c] = jnp.zeros((rows, LANES), F32)

    def reduce_state(g, t, onehot):
        for cl in range(ncg):
            c = g * ncg + cl
            p = (s_scr[c] * row(nkk_ref, c, t)).astype(BF16)
            vsel = jnp.where(onehot, vt_scr[c], 0.0).astype(BF16)
            l1[g, cl * rows:(cl + 1) * rows, :] = jnp.concatenate([p, vsel], axis=1)
        r1[g] = _dot(l1[g], ones_ref[...])

    def update(g, t):
        for cl in range(ncg):
            c = g * ncg + cl
            rs, ls = pair_slot(cl)
            sa = r1[g, cl * rows:(cl + 1) * rows, 0:LANES]
            vc = r1[g, cl * rows:(cl + 1) * rows, LANES:2 * LANES]
            s_new = s_scr[c] * row(w_ref, c, t) + sa * row(b_ref, c, t) + vc * row(k_ref, c, t)
            s_scr[c] = s_new
            l2[g, rs, ls] = (s_new * row(r_ref, c, t)).astype(BF16)
        r2[g] = _dot(l2[g], ones_ref[...])

    def collect(g, onehot):
        for cl in range(ncg):
            rs, ls = pair_slot(cl)
            c = g * ncg + cl
            yt_scr[c] = jnp.where(onehot, r2[g, rs, ls], yt_scr[c])

    def step(t, carry):
        onehot = (lane & (RW_HEAD - 1)) == t
        for g in range(SCAN_GROUPS):
            reduce_state(g, t, onehot)
        for g in range(SCAN_GROUPS):
            update(g, t)
        for g in range(SCAN_GROUPS):
            collect(g, onehot)
        return carry

    lax.fori_loop(0, tb, step, 0, unroll=2)

    for c in range(nc):
        y0 = yt_scr[c]
        yt = jnp.concatenate([jnp.where(low, y0, 0.0), jnp.where(low, pltpu.roll(y0, RW_HEAD, 1), 0.0)], axis=0)
        y_ref[c // N_CHUNK, :, lanes_of(c)] = yt.T[:tb]

    @pl.when(tblk == pl.num_programs(1) - 1)
    def _():
        for c in range(nc):
            st_ref[c // N_CHUNK, c % N_CHUNK] = s_scr[c]


def rwkv_recurrence(row_inputs, v, row0, b, t, s0, ones, bg, tb):
    d = v.shape[1]
    assert tb <= RW_HEAD and t % tb == 0 and b % bg == 0 and row0 % tb == 0
    nc = bg * N_CHUNK
    ncg = nc // SCAN_GROUPS
    tiles = t // tb

    def member(block, m):
        return pl.BlockSpec(block, lambda i, j: (row0 // tb + (i * bg + m) * tiles + j,) + (0,) * (len(block) - 1))

    seq = pl.BlockSpec((bg, tb, d), lambda i, j: (i, j, 0))
    state = pl.BlockSpec((bg, N_CHUNK, RW_HEAD, LANES), lambda i, j: (i, 0, 0, 0))
    operands, specs = [], []
    for a in row_inputs:
        operands += [a] * bg
        specs += [member((tb, N_CHUNK, LANES), m) for m in range(bg)]
    operands += [v] * bg
    specs += [member((tb, d), m) for m in range(bg)]
    return pl.pallas_call(
        functools.partial(_scan_kernel, bg, tb),
        grid=(b // bg, tiles),
        in_specs=specs + [state, pl.BlockSpec((2 * LANES, 2 * LANES), lambda i, j: (0, 0))],
        out_specs=[seq, state],
        out_shape=[jax.ShapeDtypeStruct((b, t, d), F32), jax.ShapeDtypeStruct(s0.shape, F32)],
        scratch_shapes=[pltpu.VMEM((nc, RW_HEAD, LANES), F32),
                        pltpu.VMEM((nc, RW_HEAD, LANES), F32),
                        pltpu.VMEM((nc, RW_HEAD, LANES), F32),
                        pltpu.VMEM((SCAN_GROUPS, ncg * RW_HEAD, 2 * LANES), BF16),
                        pltpu.VMEM((SCAN_GROUPS, ncg * RW_HEAD, 2 * LANES), F32),
                        pltpu.VMEM((SCAN_GROUPS, ncg * RW_HEAD // 2, 2 * LANES), BF16),
                        pltpu.VMEM((SCAN_GROUPS, ncg * RW_HEAD // 2, 2 * LANES), F32)],
        compiler_params=_params("parallel", "arbitrary"),
        name="scan_rwkv",
    )(*operands, s0, ones)


SUBLANES = 8


def _sort_network(n):
    pairs, p = [], 1
    while p < n:
        k = p
        while k >= 1:
            for j in range(k % p, n - k, 2 * k):
                for i in range(min(k, n - j - k)):
                    if (i + j) // (2 * p) == (i + j + k) // (2 * p):
                        pairs.append((i + j, i + j + k))
            k //= 2
        p *= 2
    return pairs


def _exchange(v, i, j):
    v[i], v[j] = jnp.maximum(v[i], v[j]), jnp.minimum(v[i], v[j])


def _bitonic_merge(v):
    k = len(v) // 2
    while k >= 1:
        for i in range(len(v)):
            if not i & k:
                _exchange(v, i, i + k)
        k //= 2


def _merge_sublanes(v):
    for shift in (4, 2, 1):
        w = [pltpu.roll(x, shift, 0) for x in v]
        if len(v) < PK_TOPK:
            v = v + w[::-1]
        else:
            v = [jnp.maximum(v[i], w[PK_TOPK - 1 - i]) for i in range(PK_TOPK)]
        _bitonic_merge(v)
    return v


def _top16_of_rows(s):
    v = [s[SUBLANES * i:SUBLANES * (i + 1), :] for i in range(s.shape[0] // SUBLANES)]
    for i, j in _sort_network(len(v)):
        _exchange(v, i, j)
    return _merge_sublanes(v)


_CAND_PAIRS = [(a, b) for a in range(PK_TOPK) for b in range(PK_TOPK) if (a + 1) * (b + 1) <= PK_TOPK]


def _top16_pair_sums(t1, t2):
    sub = lax.broadcasted_iota(jnp.int32, t1[0].shape, 0)
    packed = []
    for k in range(0, len(_CAND_PAIRS), SUBLANES):
        tile = jnp.full(t1[0].shape, -jnp.inf, F32)
        for r, (a, b) in enumerate(_CAND_PAIRS[k:k + SUBLANES]):
            tile = jnp.where(sub == r, t1[a] + t2[b], tile)
        packed.append(tile)
    while len(packed) & (len(packed) - 1):
        packed.append(jnp.full(t1[0].shape, -jnp.inf, F32))
    for i, j in _sort_network(len(packed)):
        _exchange(packed, i, j)
    return _merge_sublanes(packed)


def _pquery_kernel(xn_ref, wq_ref, keys_ref, st_ref, thr_ref):
    q = _dot(xn_ref[...], wq_ref[...])
    for c in range(2 * PK_HEADS):
        qc = q[:, c * LANES:(c + 1) * LANES].astype(BF16)
        st_ref[c * LANES:(c + 1) * LANES, :] = lax.dot_general(
            keys_ref[c], qc, (((1,), (1,)), ((), ())), preferred_element_type=F32)

    for h in range(PK_HEADS):
        r1 = slice((2 * h) * LANES, (2 * h + 1) * LANES)
        r2 = slice((2 * h + 1) * LANES, (2 * h + 2) * LANES)
        s1, s2 = st_ref[r1, :], st_ref[r2, :]
        t1 = _top16_of_rows(s1)
        t2 = _top16_of_rows(s2)
        top = _top16_pair_sums(t1, t2)
        z = jnp.zeros_like(top[0])
        for m in top:
            z = z + jnp.exp(m - top[0])
        off = top[0] + jnp.log(z) + LN2
        st_ref[r1, :] = (s1 - off[0:1, :]) * LOG2E
        st_ref[r2, :] = s2 * LOG2E
        top = _top16_pair_sums([(t - off) * LOG2E for t in t1], [t * LOG2E for t in t2])
        thr_ref[h:h + 1, :] = top[-1][0:1, :]


def peer_query(xn, w_q, keys):
    n, d = xn.shape
    tm = _row_tile(n, PQ_TOKEN_TILE)
    return pl.pallas_call(
        _pquery_kernel,
        grid=(n // tm,),
        in_specs=[pl.BlockSpec((tm, d), lambda i: (i, 0)), pl.BlockSpec((d, d), lambda i: (0, 0)),
                  pl.BlockSpec(keys.shape, lambda i: (0, 0, 0))],
        out_specs=[pl.BlockSpec((d, tm), lambda i: (0, i)), pl.BlockSpec((PK_HEADS, tm), lambda i: (0, i))],
        out_shape=[jax.ShapeDtypeStruct((d, n), F32), jax.ShapeDtypeStruct((PK_HEADS, n), F32)],
        compiler_params=_params("parallel"),
        name="peer_query",
    )(xn, w_q, keys)


PD_PARTS = 4


def _pdense_kernel(ec, xn_ref, u_ref, vt_ref, st_ref, thr_ref, y_ref, acc, xt_scr, hid_scr):
    j = pl.program_id(1)
    groups = ec // PK_NKEYS
    sub = 8

    @pl.when(j == 0)
    def _():
        acc[...] = jnp.zeros(acc.shape, F32)
        xt_scr[...] = xn_ref[...].astype(F32).T.astype(BF16)

    tm = xn_ref.shape[0]
    gpp = groups // PD_PARTS
    rows = gpp * PK_NKEYS
    thr = [jnp.broadcast_to(thr_ref[h:h + 1, :], (sub, tm)) for h in range(PK_HEADS)]

    def hidden(part):
        r = slice(part * rows, (part + 1) * rows)
        hid_scr[r, :] = _dot(u_ref[r, :], xt_scr[...])

    def finish(part):
        ps = []
        for ii in range(part * gpp, (part + 1) * gpp):
            gates = [None] * (PK_NKEYS // sub)
            for h in range(PK_HEADS):
                s1 = jnp.broadcast_to(st_ref[pl.ds(2 * h * LANES + j * groups + ii, 1), :], (sub, tm))
                for s in range(PK_NKEYS // sub):
                    c = s1 + st_ref[(2 * h + 1) * LANES + s * sub:(2 * h + 1) * LANES + (s + 1) * sub, :]
                    e = jnp.where(c >= thr[h], jnp.exp2(c), 0.0)
                    gates[s] = e if gates[s] is None else gates[s] + e
            hh = hid_scr[ii * PK_NKEYS:(ii + 1) * PK_NKEYS, :]
            ps.append((hh * (1.0 + lax.erf(hh * INV_SQRT2)) * jnp.concatenate(gates, axis=0)).astype(BF16))
        acc[...] += _dot(vt_ref[:, part * rows:(part + 1) * rows], jnp.concatenate(ps, axis=0))

    hidden(0)
    for part in range(PD_PARTS):
        if part + 1 < PD_PARTS:
            hidden(part + 1)
        finish(part)

    @pl.when(j == pl.num_programs(1) - 1)
    def _():
        y_ref[...] = acc[...].T


def peer_dense(xn, u, v_t, st, thr):
    n, d = xn.shape
    tm = _row_tile(n, PD_TOKEN_TILE)
    ec = PD_EXPERT_CHUNK
    return pl.pallas_call(
        functools.partial(_pdense_kernel, ec),
        grid=(n // tm, PK_EXPERTS // ec),
        in_specs=[pl.BlockSpec((tm, d), lambda i, j: (i, 0)),
                  pl.BlockSpec((ec, d), lambda i, j: (j, 0)),
                  pl.BlockSpec((d, ec), lambda i, j: (0, j)),
                  pl.BlockSpec((d, tm), lambda i, j: (0, i)),
                  pl.BlockSpec((PK_HEADS, tm), lambda i, j: (0, i))],
        out_specs=pl.BlockSpec((tm, d), lambda i, j: (i, 0)),
        out_shape=jax.ShapeDtypeStruct((n, d), F32),
        scratch_shapes=[pltpu.VMEM((d, tm), F32), pltpu.VMEM((d, tm), BF16), pltpu.VMEM((ec, tm), F32)],
        compiler_params=_params("parallel", "arbitrary"),
        name="peer_dense",
    )(xn, u, v_t, st, thr)


def peer(xn, w_q, keys, u, v_t):
    st, thr = peer_query(xn, w_q, keys)
    return peer_dense(xn, u, v_t, st, thr)


def _pack_rwkv_state(s):
    b = s.shape[0]
    return s.reshape(b, N_CHUNK, 2, RW_HEAD, RW_HEAD).transpose(0, 1, 3, 2, 4).reshape(b, N_CHUNK, RW_HEAD, LANES)


def _unpack_rwkv_state(s):
    b = s.shape[0]
    return s.reshape(b, N_CHUNK, RW_HEAD, 2, RW_HEAD).transpose(0, 1, 3, 2, 4).reshape(b, 2 * N_CHUNK, RW_HEAD, RW_HEAD)


def _shifted(h, shift0):
    return jnp.concatenate([shift0[:, None].astype(h.dtype), h[:, :-1]], axis=1)


def kernel(x_prompt, x_sample, state_rwkv_wkv, state_rwkv_shift, state_hgrn, ln_mix_w, ln_ffn_w, ln_f_w, rw_mu, rw_w_rkv, rw_w0, rw_w1, rw_w2, rw_a0, rw_a1, rw_a2, rw_g1, rw_g2, rw_k_k, rw_k_a, rw_r_k, rw_ln_w, rw_ln_b, rw_w_o, hg_w_in, hg_lb, hg_norm_w, hg_w_o, pk_w_q, pk_keys, pk_u, pk_v):
    bp, tp, d = x_prompt.shape
    bs, ts, _ = x_sample.shape
    n_p, n_s = bp * tp, bs * ts
    ones64 = _block_ones(RW_HEAD)
    ones128 = _block_ones(HG_HEAD)
    ones64x4 = _block_ones(RW_HEAD, 2 * LANES)

    def split(a):
        return a[:n_p].reshape(bp, tp, d), a[n_p:].reshape(bs, ts, d)

    x = jnp.concatenate([x_prompt.reshape(n_p, d), x_sample.reshape(n_s, d)], axis=0)

    h = rmsnorm(x, ln_mix_w[0])
    h_p, h_s = split(h)
    xp = jnp.concatenate([_shifted(h_p, jnp.zeros((bp, d), F32)).reshape(n_p, d),
                          _shifted(h_s, state_rwkv_shift[0]).reshape(n_s, d)], axis=0)
    rkv = rwkv_rkv(h, xp, rw_mu[0, :3], rw_w_rkv[0].astype(BF16))
    lora_pad = LANES - rw_w1.shape[-1]
    pad_in = lambda w: jnp.pad(w, ((0, 0), (0, lora_pad))).astype(BF16)
    pad_out = lambda w: jnp.pad(w, ((0, lora_pad), (0, 0))).astype(BF16)
    dec, nkk, bb, k2, g, rkb = rwkv_prep(
        h, xp, rkv[0], rkv[1], rw_mu[0, 3:], pad_in(rw_w1[0]), pad_out(rw_w2[0]), pad_in(rw_a1[0]),
        pad_out(rw_a2[0]), rw_g1[0].astype(BF16), rw_g2[0].astype(BF16), rw_w0[0], rw_a0[0], rw_k_k[0],
        rw_k_a[0], rw_r_k[0].reshape(d), ones64)

    rows4 = [a.reshape(n_p + n_s, N_CHUNK, LANES) for a in (dec, nkk, bb, k2, rkv[0])]
    y_p, wkv_p = rwkv_recurrence(rows4, rkv[2], 0, bp, tp, jnp.zeros((bp, N_CHUNK, RW_HEAD, LANES), F32),
                                 ones64x4, SCAN_SEQS, min(SCAN_STEPS, tp))
    y_s, wkv_s = rwkv_recurrence(rows4, rkv[2], n_p, bs, ts, _pack_rwkv_state(state_rwkv_wkv[0]),
                                 ones64x4, SCAN_SEQS, min(SCAN_STEPS, ts))
    y = jnp.concatenate([y_p.reshape(n_p, d), y_s.reshape(n_s, d)], axis=0)
    x, xn = rwkv_post(y, rkv[2], rkb, g, x, rw_ln_w[0], rw_ln_b[0], rw_w_o[0].astype(BF16), ln_ffn_w[0], ones64)
    yp = peer(xn, pk_w_q[0].astype(BF16), pk_keys[0].reshape(2 * PK_HEADS, PK_NKEYS, LANES).astype(BF16),
              pk_u[0].astype(BF16), pk_v[0].astype(BF16).T)
    x, h1 = add_norm(x, yp, ln_mix_w[1], BF16)

    lb = jax.nn.softmax(hg_lb.astype(F32), axis=0)
    lb = (jnp.cumsum(lb, axis=0) - lb[0])[1]
    proj = hgrn_in(h1, hg_w_in[0].astype(BF16), lb)
    q, f, vin, sgate = proj[0], proj[1], proj[2], proj[3]

    o_p, hg_p = hgrn_chunked(f, q, vin, jnp.zeros((bp, N_CHUNK, HG_HEAD, LANES), F32), 0, bp, tp, HG_CHUNK, LANES,
                             ones128)
    o_s, hg_s = hgrn_chunked(f, q, vin, jnp.swapaxes(state_hgrn[0], -1, -2), n_p, bs, ts, ts, 8 * ts, ones128)
    o = jnp.concatenate([o_p, o_s], axis=0)
    x, xn = hgrn_post(o, sgate, x, jnp.tile(hg_norm_w[0], N_CHUNK), hg_w_o[0].astype(BF16), ln_ffn_w[1], ones128)
    yp = peer(xn, pk_w_q[1].astype(BF16), pk_keys[1].reshape(2 * PK_HEADS, PK_NKEYS, LANES).astype(BF16),
              pk_u[1].astype(BF16), pk_v[1].astype(BF16).T)
    _, out = add_norm(x, yp, ln_f_w, F32)

    y_p, y_s = split(out)
    return (y_p, y_s,
            _unpack_rwkv_state(wkv_p)[None], h_p[:, -1][None], jnp.swapaxes(hg_p, -1, -2)[None],
            _unpack_rwkv_state(wkv_s)[None], h_s[:, -1][None], jnp.swapaxes(hg_s, -1, -2)[None])
```
